```python
import math
import jax, jax.numpy as jnp
from jax import lax
import numpy as np

D_MODEL = 1024
BATCH = 16
SEQ = 2048
DEPTH = 1

D_SSM = D_MODEL // 2
SSM_GROUP = 16
N_SSM_GROUPS = D_SSM // SSM_GROUP
SSM_STATE = 64
DT_MIN = 0.001
DT_MAX = 0.1
D_CONV = D_MODEL // 2
CONV_WIDTH = 31
N_BRANCH = 2
SPLITS = [D_SSM, D_SSM + D_CONV, D_SSM + 2 * D_CONV, D_SSM + 2 * D_CONV + D_MODEL]
D_IN = D_SSM + 2 * D_CONV + N_BRANCH * D_MODEL
N_GROUPS = 4
N_EXP_PER_GROUP = 8
N_EXPERTS = N_GROUPS * N_EXP_PER_GROUP
TOP_K_INNER = 2
D_FF_EXPERT = D_MODEL // 4
ROW_BLOCK = 128
LN_EPS = 1e-5
ALPHA = (2.0 * DEPTH) ** 0.25
BETA = (8.0 * DEPTH) ** -0.25

kernel_name = "hybrid_s5_conformer_hmoe_deepnorm"


def _layer_norm(x, g, b):
    xf = x.astype(jnp.float32)
    mu = jnp.mean(xf, axis=-1, keepdims=True)
    var = jnp.mean(jnp.square(xf - mu), axis=-1, keepdims=True)
    return ((xf - mu) * lax.rsqrt(var + LN_EPS)).astype(x.dtype) * g + b


def _s5_combine(e1, e2):
    a1r, a1i, b1r, b1i = e1
    a2r, a2i, b2r, b2i = e2
    return (a2r * a1r - a2i * a1i,
            a2r * a1i + a2i * a1r,
            a2r * b1r - a2i * b1i + b2r,
            a2r * b1i + a2i * b1r + b2i)


def _s5_mixer(u, lam_re, lam_im, log_dt, b_re, b_im, c_re, c_im, d_skip):
    bsz, seq, _ = u.shape
    f32 = jnp.float32
    lr, li = lam_re.astype(f32), lam_im.astype(f32)
    dt = jnp.exp(log_dt.astype(f32))[:, None]
    mag = jnp.exp(lr * dt)
    ar, ai = mag * jnp.cos(li * dt), mag * jnp.sin(li * dt)
    den = lr * lr + li * li
    fr = ((ar - 1.0) * lr + ai * li) / den
    fi = (ai * lr - (ar - 1.0) * li) / den
    br, bi = b_re.astype(f32), b_im.astype(f32)
    bbar_re = fr[..., None] * br - fi[..., None] * bi
    bbar_im = fr[..., None] * bi + fi[..., None] * br
    ug = u.astype(f32).reshape(bsz, seq, N_SSM_GROUPS, SSM_GROUP)
    bu_re = jnp.einsum('blgh,gph->blgp', ug, bbar_re)
    bu_im = jnp.einsum('blgh,gph->blgp', ug, bbar_im)
    a_shape = (1, seq, N_SSM_GROUPS, SSM_STATE)
    a_re = jnp.broadcast_to(ar[None, None], a_shape)
    a_im = jnp.broadcast_to(ai[None, None], a_shape)
    _, _, s_re, s_im = lax.associative_scan(_s5_combine, (a_re, a_im, bu_re, bu_im), axis=1)
    y = (jnp.einsum('blgp,ghp->blgh', s_re, c_re.astype(f32))
         - jnp.einsum('blgp,ghp->blgh', s_im, c_im.astype(f32)))
    y = y.reshape(bsz, seq, D_SSM).astype(u.dtype)
    return y + d_skip * u


def _conv_mixer(ca, cb, conv_w, conv_b, ln_g, ln_b, w_cout, b_cout):
    v = ca * jax.nn.sigmoid(cb)
    vp = jnp.pad(v, ((0, 0), (CONV_WIDTH - 1, 0), (0, 0)))
    c = lax.conv_general_dilated(vp, conv_w, window_strides=(1,), padding='VALID',
                                 dimension_numbers=('NWC', 'WIO', 'NWC'),
                                 feature_group_count=D_CONV) + conv_b
    c = jax.nn.silu(_layer_norm(c, ln_g, ln_b))
    return c @ w_cout + b_cout


def _hier_moe(h, w_rg, b_rg, w_re, b_re, w_gate, w_up, w_down):
    bsz, seq, d = h.shape
    xt = h.reshape(bsz * seq, d)
    n_tok = xt.shape[0]
    zg = (xt @ w_rg + b_rg).astype(jnp.float32)
    pg = jax.nn.softmax(zg, axis=-1)
    _, gsel = lax.top_k(zg, 1)
    p_group = jnp.take_along_axis(pg, gsel, axis=-1)
    ze = (xt @ w_re + b_re).astype(jnp.float32).reshape(n_tok, N_GROUPS, N_EXP_PER_GROUP)
    ze = jnp.take_along_axis(ze, gsel[:, :, None], axis=1)[:, 0]
    top_v, top_i = lax.top_k(ze, TOP_K_INNER)
    gate = (p_group * jax.nn.softmax(top_v, axis=-1)).astype(h.dtype)
    eid = (gsel * N_EXP_PER_GROUP + top_i).reshape(-1)
    wts = gate.reshape(-1)
    tok = jnp.repeat(jnp.arange(n_tok, dtype=jnp.int32), TOP_K_INNER)
    n_assign = n_tok * TOP_K_INNER
    order = jnp.argsort(eid)
    s_eid, s_tok, s_w = eid[order], tok[order], wts[order]
    counts = jnp.bincount(eid, length=N_EXPERTS)
    start = jnp.cumsum(counts) - counts
    pcounts = (counts + ROW_BLOCK - 1) // ROW_BLOCK * ROW_BLOCK
    pend = jnp.cumsum(pcounts)
    dest = (pend - pcounts)[s_eid] + jnp.arange(n_assign, dtype=jnp.int32) - start[s_eid]
    n_blocks = -(-n_assign // ROW_BLOCK) + N_EXPERTS
    n_rows = n_blocks * ROW_BLOCK
    tok_buf = jnp.zeros((n_rows,), jnp.int32).at[dest].set(s_tok)
    w_buf = jnp.zeros((n_rows,), h.dtype).at[dest].set(s_w)
    blk_start = jnp.arange(n_blocks, dtype=jnp.int32) * ROW_BLOCK
    blk_eid = jnp.minimum(jnp.sum(blk_start[:, None] >= pend[None, :], axis=1), N_EXPERTS - 1)

    def expert_block(args):
        idx, e = args
        xb = xt[idx]
        hid = jax.nn.silu(xb @ w_gate[e]) * (xb @ w_up[e])
        return hid @ w_down[e]

    y_buf = lax.map(expert_block, (tok_buf.reshape(n_blocks, ROW_BLOCK), blk_eid))
    y = jax.ops.segment_sum(y_buf.reshape(n_rows, d) * w_buf[:, None], tok_buf, num_segments=n_tok)
    return y.reshape(bsz, seq, d)


def _layer(x, w_in, b_in, lam_re, lam_im, log_dt, ssm_b_re, ssm_b_im, ssm_c_re, ssm_c_im,
           ssm_d, w_glu, b_glu, conv_w, conv_b, ln_c_g, ln_c_b, w_cout, b_cout, w_out, b_out,
           ln1_g, ln1_b, w_route_group, b_route_group, w_route_expert, b_route_expert,
           w_gate, w_up, w_down, ln2_g, ln2_b):
    proj = x @ w_in + b_in
    u, ca, cb, gs, gc = jnp.split(proj, SPLITS, axis=-1)
    y = jax.nn.gelu(_s5_mixer(u, lam_re, lam_im, log_dt, ssm_b_re, ssm_b_im,
                              ssm_c_re, ssm_c_im, ssm_d))
    glu = y @ w_glu + b_glu
    s_out = glu[..., :D_MODEL] * jax.nn.sigmoid(glu[..., D_MODEL:])
    c_out = _conv_mixer(ca, cb, conv_w, conv_b, ln_c_g, ln_c_b, w_cout, b_cout)
    merged = jax.nn.sigmoid(gs) * s_out + jax.nn.sigmoid(gc) * c_out
    mix = merged @ w_out + b_out
    h = _layer_norm(ALPHA * x + mix, ln1_g, ln1_b)
    moe = _hier_moe(h, w_route_group, b_route_group, w_route_expert, b_route_expert,
                    w_gate, w_up, w_down)
    return _layer_norm(ALPHA * h + moe, ln2_g, ln2_b)


def _nrm(k, shape, scale):
    return scale * jax.random.normal(k, shape, jnp.float32)


def setup_inputs(seed: int = 0) -> dict:
    key = jax.random.key(seed)
    ks = jax.random.split(key, 32)
    L = DEPTH
    G, P, H = N_SSM_GROUPS, SSM_STATE, SSM_GROUP
    lam_im = jnp.pi * jnp.arange(P, dtype=jnp.float32)
    return {
        'x': _nrm(ks[0], (BATCH, SEQ, D_MODEL), 1.0),
        'w_in': _nrm(ks[1], (L, D_MODEL, D_IN), D_MODEL ** -0.5),
        'b_in': _nrm(ks[2], (L, D_IN), 0.01),
        'lam_re': -0.5 + _nrm(ks[3], (L, G, P), 0.01),
        'lam_im': lam_im + _nrm(ks[4], (L, G, P), 0.01),
        'log_dt': jax.random.uniform(ks[5], (L, G), jnp.float32, math.log(DT_MIN), math.log(DT_MAX)),
        'ssm_b_re': _nrm(ks[6], (L, G, P, H), (2.0 * H) ** -0.5),
        'ssm_b_im': _nrm(ks[7], (L, G, P, H), (2.0 * H) ** -0.5),
        'ssm_c_re': _nrm(ks[8], (L, G, H, P), (2.0 * P) ** -0.5),
        'ssm_c_im': _nrm(ks[9], (L, G, H, P), (2.0 * P) ** -0.5),
        'ssm_d': _nrm(ks[10], (L, D_SSM), 1.0),
        'w_glu': _nrm(ks[11], (L, D_SSM, 2 * D_MODEL), D_SSM ** -0.5),
        'b_glu': _nrm(ks[12], (L, 2 * D_MODEL), 0.01),
        'conv_w': _nrm(ks[13], (L, CONV_WIDTH, 1, D_CONV), CONV_WIDTH ** -0.5),
        'conv_b': _nrm(ks[14], (L, D_CONV), 0.01),
        'ln_c_g': 1.0 + _nrm(ks[15], (L, D_CONV), 0.01),
        'ln_c_b': _nrm(ks[16], (L, D_CONV), 0.01),
        'w_cout': _nrm(ks[17], (L, D_CONV, D_MODEL), D_CONV ** -0.5),
        'b_cout': _nrm(ks[18], (L, D_MODEL), 0.01),
        'w_out': _nrm(ks[19], (L, D_MODEL, D_MODEL), BETA * D_MODEL ** -0.5),
        'b_out': _nrm(ks[20], (L, D_MODEL), 0.01),
        'ln1_g': 1.0 + _nrm(ks[21], (L, D_MODEL), 0.01),
        'ln1_b': _nrm(ks[22], (L, D_MODEL), 0.01),
        'w_route_group': _nrm(ks[23], (L, D_MODEL, N_GROUPS), D_MODEL ** -0.5),
        'b_route_group': _nrm(ks[24], (L, N_GROUPS), 0.01),
        'w_route_expert': _nrm(ks[25], (L, D_MODEL, N_EXPERTS), D_MODEL ** -0.5),
        'b_route_expert': _nrm(ks[26], (L, N_EXPERTS), 0.01),
        'w_gate': _nrm(ks[27], (L, N_EXPERTS, D_MODEL, D_FF_EXPERT), D_MODEL ** -0.5),
        'w_up': _nrm(ks[28], (L, N_EXPERTS, D_MODEL, D_FF_EXPERT), D_MODEL ** -0.5),
        'w_down': _nrm(ks[29], (L, N_EXPERTS, D_FF_EXPERT, D_MODEL), BETA * D_FF_EXPERT ** -0.5),
        'ln2_g': 1.0 + _nrm(ks[30], (L, D_MODEL), 0.01),
        'ln2_b': _nrm(ks[31], (L, D_MODEL), 0.01),
    }


def reference(x, w_in, b_in, lam_re, lam_im, log_dt, ssm_b_re, ssm_b_im, ssm_c_re, ssm_c_im,
              ssm_d, w_glu, b_glu, conv_w, conv_b, ln_c_g, ln_c_b, w_cout, b_cout, w_out, b_out,
              ln1_g, ln1_b, w_route_group, b_route_group, w_route_expert, b_route_expert,
              w_gate, w_up, w_down, ln2_g, ln2_b):
    params = (w_in, b_in, lam_re, lam_im, log_dt, ssm_b_re, ssm_b_im, ssm_c_re, ssm_c_im,
              ssm_d, w_glu, b_glu, conv_w, conv_b, ln_c_g, ln_c_b, w_cout, b_cout, w_out, b_out,
              ln1_g, ln1_b, w_route_group, b_route_group, w_route_expert, b_route_expert,
              w_gate, w_up, w_down, ln2_g, ln2_b)
    h = x
    for i in range(DEPTH):
        h = _layer(h, *[p[i] for p in params])
    return h
```

```python
import functools
import math

import jax
import jax.numpy as jnp
from jax import lax
from jax.experimental import pallas as pl
from jax.experimental.pallas import tpu as pltpu

F32 = jnp.float32
BF16 = jnp.bfloat16
HIGHEST = lax.Precision.HIGHEST

D_MODEL = 1024
D_SSM = 512
SSM_GROUP = 16
N_SSM_GROUPS = D_SSM // SSM_GROUP
SSM_STATE = 64
D_CONV = 512
CONV_WIDTH = 31
N_GROUPS = 4
N_EXP_PER_GROUP = 8
N_EXPERTS = N_GROUPS * N_EXP_PER_GROUP
TOP_K_INNER = 2
D_FF_EXPERT = D_MODEL // 4
LN_EPS = 1e-5

CHUNK = 16
CHUNK_F = CHUNK * SSM_GROUP
STATE_F = 2 * SSM_STATE
ROUTE_LANES = 128
HALO = 32
VMEM_LIMIT = 56 * 1024 * 1024


def _layer_norm(x, g, b):
    mu = jnp.mean(x, axis=-1, keepdims=True)
    xc = x - mu
    var = jnp.mean(xc * xc, axis=-1, keepdims=True)
    return xc * lax.rsqrt(var + LN_EPS) * g + b


def _inproj_kernel(x_ref, w_ref, b_ref, u_ref, v_ref):
    x = x_ref[...].astype(BF16)
    p = jnp.dot(x, w_ref[...], preferred_element_type=F32) + b_ref[...]
    u_ref[...] = p[:, :D_SSM]
    ca = p[:, D_SSM:D_SSM + D_CONV]
    cb = p[:, D_SSM + D_CONV:]
    v_ref[...] = ca * jax.nn.sigmoid(cb)


def _inproj(x2, w_ucv, b_ucv, tm):
    n_tok = x2.shape[0]
    n_col = w_ucv.shape[1]
    return pl.pallas_call(
        _inproj_kernel,
        grid=(n_tok // tm,),
        in_specs=[
            pl.BlockSpec((tm, D_MODEL), lambda i: (i, 0)),
            pl.BlockSpec((D_MODEL, n_col), lambda i: (0, 0)),
            pl.BlockSpec((1, n_col), lambda i: (0, 0)),
        ],
        out_specs=[
            pl.BlockSpec((tm, D_SSM), lambda i: (i, 0)),
            pl.BlockSpec((tm, D_CONV), lambda i: (i, 0)),
        ],
        out_shape=[
            jax.ShapeDtypeStruct((n_tok, D_SSM), F32),
            jax.ShapeDtypeStruct((n_tok, D_CONV), F32),
        ],
        compiler_params=pltpu.CompilerParams(
            dimension_semantics=("parallel",), vmem_limit_bytes=VMEM_LIMIT),
        name="inproj",
    )(x2, w_ucv, b_ucv)


def _s5_chunk_weights(lam_re, lam_im, log_dt, b_re, b_im, c_re, c_im, n_steps):
    f32 = F32
    lr, li = lam_re.astype(f32), lam_im.astype(f32)
    dt = jnp.exp(log_dt.astype(f32))[:, None]
    mag = jnp.exp(lr * dt)
    ar, ai = mag * jnp.cos(li * dt), mag * jnp.sin(li * dt)
    den = lr * lr + li * li
    fr = ((ar - 1.0) * lr + ai * li) / den
    fi = (ai * lr - (ar - 1.0) * li) / den
    br, bi = b_re.astype(f32), b_im.astype(f32)
    bbr = fr[..., None] * br - fi[..., None] * bi
    bbi = fr[..., None] * bi + fi[..., None] * br
    cr, ci = c_re.astype(f32), c_im.astype(f32)

    pr, pi = [jnp.ones_like(ar)], [jnp.zeros_like(ai)]
    for _ in range(CHUNK):
        pr_n = pr[-1] * ar - pi[-1] * ai
        pi_n = pr[-1] * ai + pi[-1] * ar
        pr.append(pr_n)
        pi.append(pi_n)
    pw_r, pw_i = jnp.stack(pr), jnp.stack(pi)

    ca_r = cr[None] * pw_r[:, :, None, :] - ci[None] * pw_i[:, :, None, :]
    ca_i = cr[None] * pw_i[:, :, None, :] + ci[None] * pw_r[:, :, None, :]
    kk = (jnp.einsum('kghp,gpq->kghq', ca_r[:CHUNK], bbr, precision=HIGHEST)
          - jnp.einsum('kghp,gpq->kghq', ca_i[:CHUNK], bbi, precision=HIGHEST))
    t_in = jnp.arange(CHUNK)[:, None]
    t_out = jnp.arange(CHUNK)[None, :]
    lag = t_out - t_in
    m = kk[jnp.maximum(lag, 0)]
    m = jnp.where((lag >= 0)[:, :, None, None, None], m, 0.0)
    m = m.transpose(2, 0, 4, 1, 3).reshape(N_SSM_GROUPS, CHUNK_F, CHUNK_F)

    rp_r, rp_i = pw_r[CHUNK - 1::-1][:CHUNK], pw_i[CHUNK - 1::-1][:CHUNK]
    z_r = rp_r[..., None] * bbr[None] - rp_i[..., None] * bbi[None]
    z_i = rp_r[..., None] * bbi[None] + rp_i[..., None] * bbr[None]
    z = jnp.concatenate([z_r, z_i], axis=2)
    z = z.transpose(1, 0, 3, 2).reshape(N_SSM_GROUPS, CHUNK_F, STATE_F)
    mz = jnp.concatenate([m, z], axis=-1)

    w = jnp.concatenate([ca_r[1:], -ca_i[1:]], axis=-1)
    w = w.transpose(1, 3, 0, 2).reshape(N_SSM_GROUPS, STATE_F, CHUNK_F)

    sr, si = pw_r[CHUNK], pw_i[CHUNK]
    c1, c2 = [], []
    for _ in range(n_steps):
        c1.append(jnp.concatenate([sr, sr], axis=-1)[None])
        c2.append(jnp.concatenate([-si, si], axis=-1)[None])
        sr, si = sr * sr - si * si, 2.0 * sr * si
    pad = jnp.zeros((8 - n_steps, N_SSM_GROUPS, STATE_F), f32)
    coef = jnp.concatenate(c1 + [pad] + c2 + [pad], axis=0).transpose(1, 0, 2)
    return mz.astype(BF16), w.astype(BF16), coef


def _s5_kernel(u_ref, mz_ref, w_ref, coef_ref, y_ref, *, nj, n_steps):
    u = u_ref[0]
    r = jnp.dot(u, mz_ref[0], preferred_element_type=F32)
    y0 = r[:, :CHUNK_F]
    t = r[:, CHUNK_F:]
    j = lax.broadcasted_iota(jnp.int32, t.shape, 0) & (nj - 1)
    for m in range(n_steps):
        d = 1 << m
        sh = jnp.where(j >= d, pltpu.roll(t, d, axis=0), 0.0)
        p1 = coef_ref[0, m:m + 1, :]
        p2 = coef_ref[0, 8 + m:9 + m, :]
        t = t + p1 * sh + p2 * pltpu.roll(sh, SSM_STATE, axis=1)
    s_prev = jnp.where(j >= 1, pltpu.roll(t, 1, axis=0), 0.0)
    y_ref[0] = y0 + jnp.dot(s_prev.astype(BF16), w_ref[0], preferred_element_type=F32)


def _s5(u_im, mz, w, coef, nj, n_steps):
    g, r, _ = u_im.shape
    return pl.pallas_call(
        functools.partial(_s5_kernel, nj=nj, n_steps=n_steps),
        grid=(g,),
        in_specs=[
            pl.BlockSpec((1, r, CHUNK_F), lambda i: (i, 0, 0)),
            pl.BlockSpec((1, CHUNK_F, CHUNK_F + STATE_F), lambda i: (i, 0, 0)),
            pl.BlockSpec((1, STATE_F, CHUNK_F), lambda i: (i, 0, 0)),
            pl.BlockSpec((1, 16, STATE_F), lambda i: (i, 0, 0)),
        ],
        out_specs=pl.BlockSpec((1, r, CHUNK_F), lambda i: (i, 0, 0)),
        out_shape=jax.ShapeDtypeStruct((g, r, CHUNK_F), F32),
        compiler_params=pltpu.CompilerParams(
            dimension_semantics=("parallel",), vmem_limit_bytes=VMEM_LIMIT),
        name="s5",
    )(u_im, mz, w, coef)


def _mix_kernel(x_ref, u_ref, y_ref, v_ref, vh_ref,
                wg_ref, bg_ref, dskip_ref, wglu_ref, bglu_ref,
                cw_ref, cb_ref, lcg_ref, lcb_ref, wco_ref, bco_ref,
                wo_ref, bo_ref, l1g_ref, l1b_ref, wrh_ref, wrl_ref, br_ref,
                h_ref, route_ref, vext_ref, conv_ref, *, tl, sub, alpha):
    i = pl.program_id(1)

    vext_ref[0:HALO, :] = jnp.where(i > 0, vh_ref[...], 0.0)
    vext_ref[HALO:, :] = v_ref[...]
    off = HALO - (CONV_WIDTH - 1)
    for s in range(tl // sub):
        acc = jnp.zeros((sub, D_CONV), F32) + cb_ref[...]
        for k in range(CONV_WIDTH):
            r0 = s * sub + off + k
            acc = acc + cw_ref[k:k + 1, :] * vext_ref[r0:r0 + sub, :]
        conv_ref[s * sub:(s + 1) * sub, :] = acc
    c = _layer_norm(conv_ref[...], lcg_ref[...], lcb_ref[...])
    c = jax.nn.silu(c)
    c_out = jnp.dot(c.astype(BF16), wco_ref[...], preferred_element_type=F32) + bco_ref[...]

    ys = jax.nn.gelu(y_ref[...] + dskip_ref[...] * u_ref[...])
    glu = jnp.dot(ys.astype(BF16), wglu_ref[...], preferred_element_type=F32) + bglu_ref[...]
    s_out = glu[:, :D_MODEL] * jax.nn.sigmoid(glu[:, D_MODEL:])

    x = x_ref[...]
    gates = jnp.dot(x.astype(BF16), wg_ref[...], preferred_element_type=F32) + bg_ref[...]
    merged = jax.nn.sigmoid(gates[:, :D_MODEL]) * s_out + jax.nn.sigmoid(gates[:, D_MODEL:]) * c_out
    mix = jnp.dot(merged.astype(BF16), wo_ref[...], preferred_element_type=F32) + bo_ref[...]
    h = _layer_norm(alpha * x + mix, l1g_ref[...], l1b_ref[...])
    h_ref[...] = h

    h_hi = h.astype(BF16)
    h_lo = (h - h_hi.astype(F32)).astype(BF16)
    z = (jnp.dot(h_hi, wrh_ref[...], preferred_element_type=F32)
         + jnp.dot(h_lo, wrh_ref[...], preferred_element_type=F32)
         + jnp.dot(h_hi, wrl_ref[...], preferred_element_type=F32)) + br_ref[...]
    lane = lax.broadcasted_iota(jnp.int32, z.shape, 1)
    lanef = lane.astype(F32)
    neg = jnp.float32(-jnp.inf)
    big = jnp.float32(1e9)
    gmask = lane < N_GROUPS
    zg = jnp.where(gmask, z, neg)
    gmax = jnp.max(zg, axis=1, keepdims=True)
    gsel = jnp.min(jnp.where(gmask & (zg == gmax), lanef, big), axis=1, keepdims=True)
    p_group = 1.0 / jnp.sum(jnp.where(gmask, jnp.exp(zg - gmax), 0.0), axis=1, keepdims=True)
    lo = N_GROUPS + N_EXP_PER_GROUP * gsel
    emask = (lanef >= lo) & (lanef < lo + N_EXP_PER_GROUP)
    ze = jnp.where(emask, z, neg)
    v1 = jnp.max(ze, axis=1, keepdims=True)
    i1 = jnp.min(jnp.where(emask & (ze == v1), lanef, big), axis=1, keepdims=True)
    emask2 = emask & (lanef != i1)
    ze2 = jnp.where(emask2, z, neg)
    v2 = jnp.max(ze2, axis=1, keepdims=True)
    i2 = jnp.min(jnp.where(emask2 & (ze2 == v2), lanef, big), axis=1, keepdims=True)
    e2 = jnp.exp(v2 - v1)
    inv = 1.0 / (1.0 + e2)
    g1 = p_group * inv
    g2 = p_group * (e2 * inv)
    out = jnp.where(lane == 0, i1 - N_GROUPS, 0.0)
    out = jnp.where(lane == 1, i2 - N_GROUPS, out)
    out = jnp.where(lane == 2, g1, out)
    out = jnp.where(lane == 3, g2, out)
    route_ref[...] = out


def _mix(x3, u3, y3, v3, wts, tl, alpha):
    b, l, _ = x3.shape
    sub = min(64, tl)
    hb = tl // HALO

    def row_spec(f):
        return pl.BlockSpec((None, tl, f), lambda bi, i: (bi, i, 0))

    def full_spec(a):
        nd = a.ndim
        return pl.BlockSpec(a.shape, lambda bi, i: (0,) * nd)

    halo_spec = pl.BlockSpec((None, HALO, D_CONV), lambda bi, i: (bi, jnp.maximum(i * hb - 1, 0), 0))
    return pl.pallas_call(
        functools.partial(_mix_kernel, tl=tl, sub=sub, alpha=alpha),
        grid=(b, l // tl),
        in_specs=[row_spec(D_MODEL), row_spec(D_SSM), row_spec(D_SSM), row_spec(D_CONV), halo_spec]
                 + [full_spec(a) for a in wts],
        out_specs=[row_spec(D_MODEL), row_spec(ROUTE_LANES)],
        out_shape=[
            jax.ShapeDtypeStruct((b, l, D_MODEL), F32),
            jax.ShapeDtypeStruct((b, l, ROUTE_LANES), F32),
        ],
        scratch_shapes=[
            pltpu.VMEM((tl + HALO, D_CONV), F32),
            pltpu.VMEM((tl, D_CONV), F32),
        ],
        compiler_params=pltpu.CompilerParams(
            dimension_semantics=("parallel", "arbitrary"), vmem_limit_bytes=VMEM_LIMIT),
        name="mix",
    )(x3, u3, y3, v3, v3, *wts)


def _moe_kernel(eid_ref, tok_ref, h_hbm, wgu_ref, wd_ref, y_ref, xbuf, sem, *, tm):
    del eid_ref

    def row_copy(r, t):
        return pltpu.make_async_copy(h_hbm.at[pl.ds(t, 1)], xbuf.at[pl.ds(r, 1)], sem)

    def issue(r, carry):
        row_copy(r, tok_ref[0, 0, r]).start()
        return carry

    def wait(r, carry):
        row_copy(r, 0).wait()
        return carry

    lax.fori_loop(0, tm, issue, 0)
    lax.fori_loop(0, tm, wait, 0)
    xb = xbuf[...].astype(BF16)
    gu = jnp.dot(xb, wgu_ref[0], preferred_element_type=F32)
    hid = jax.nn.silu(gu[:, :D_FF_EXPERT]) * gu[:, D_FF_EXPERT:]
    y_ref[...] = jnp.dot(hid.astype(BF16), wd_ref[0], preferred_element_type=F32)


def _moe(blk_eid, tok_blocks, h2, w_gu, w_d, tm):
    n_blocks = tok_blocks.shape[0]
    grid_spec = pltpu.PrefetchScalarGridSpec(
        num_scalar_prefetch=1,
        grid=(n_blocks,),
        in_specs=[
            pl.BlockSpec((1, 1, tm), lambda i, eid: (i, 0, 0), memory_space=pltpu.SMEM),
            pl.BlockSpec(memory_space=pl.ANY),
            pl.BlockSpec((1, D_MODEL, 2 * D_FF_EXPERT), lambda i, eid: (eid[i], 0, 0)),
            pl.BlockSpec((1, D_FF_EXPERT, D_MODEL), lambda i, eid: (eid[i], 0, 0)),
        ],
        out_specs=pl.BlockSpec((tm, D_MODEL), lambda i, eid: (i, 0)),
        scratch_shapes=[pltpu.VMEM((tm, D_MODEL), F32), pltpu.SemaphoreType.DMA(())],
    )
    return pl.pallas_call(
        functools.partial(_moe_kernel, tm=tm),
        grid_spec=grid_spec,
        out_shape=jax.ShapeDtypeStruct((n_blocks * tm, D_MODEL), F32),
        compiler_params=pltpu.CompilerParams(
            dimension_semantics=("arbitrary",), vmem_limit_bytes=VMEM_LIMIT),
        name="moe",
    )(blk_eid, tok_blocks, h2, w_gu, w_d)


def _combine_kernel(pos_ref, h_ref, route_ref, y_hbm, g_ref, b_ref, o_ref, buf0, buf1, sem, *, tt, alpha):
    def row_copy(buf, r, p):
        return pltpu.make_async_copy(y_hbm.at[pl.ds(p, 1)], buf.at[pl.ds(r, 1)], sem)

    def issue(r, carry):
        row_copy(buf0, r, pos_ref[0, 0, r]).start()
        row_copy(buf1, r, pos_ref[0, 1, r]).start()
        return carry

    def wait(r, carry):
        row_copy(buf0, r, 0).wait()
        row_copy(buf1, r, 0).wait()
        return carry

    lax.fori_loop(0, tt, issue, 0)
    lax.fori_loop(0, tt, wait, 0)
    w0 = route_ref[:, 2:3]
    w1 = route_ref[:, 3:4]
    moe = w0 * buf0[...] + w1 * buf1[...]
    o_ref[...] = _layer_norm(alpha * h_ref[...] + moe, g_ref[...], b_ref[...])


def _combine(pos_blocks, h2, route2, y_buf, ln_g, ln_b, tt, alpha):
    n_tok = h2.shape[0]
    return pl.pallas_call(
        functools.partial(_combine_kernel, tt=tt, alpha=alpha),
        grid=(n_tok // tt,),
        in_specs=[
            pl.BlockSpec((1, TOP_K_INNER, tt), lambda i: (i, 0, 0), memory_space=pltpu.SMEM),
            pl.BlockSpec((tt, D_MODEL), lambda i: (i, 0)),
            pl.BlockSpec((tt, ROUTE_LANES), lambda i: (i, 0)),
            pl.BlockSpec(memory_space=pl.ANY),
            pl.BlockSpec((1, D_MODEL), lambda i: (0, 0)),
            pl.BlockSpec((1, D_MODEL), lambda i: (0, 0)),
        ],
        out_specs=pl.BlockSpec((tt, D_MODEL), lambda i: (i, 0)),
        out_shape=jax.ShapeDtypeStruct((n_tok, D_MODEL), F32),
        scratch_shapes=[
            pltpu.VMEM((tt, D_MODEL), F32),
            pltpu.VMEM((tt, D_MODEL), F32),
            pltpu.SemaphoreType.DMA(()),
        ],
        compiler_params=pltpu.CompilerParams(
            dimension_semantics=("arbitrary",), vmem_limit_bytes=VMEM_LIMIT),
        name="combine",
    )(pos_blocks, h2, route2, y_buf, ln_g, ln_b)


def _dispatch(eid, tm):
    n_tok = eid.shape[0]
    n_assign = n_tok * TOP_K_INNER
    eflat = eid.reshape(-1)
    order = jnp.argsort(eflat)
    s_eid = eflat[order]
    s_tok = (order // TOP_K_INNER).astype(jnp.int32)
    counts = jnp.bincount(eflat, length=N_EXPERTS)
    start = jnp.cumsum(counts) - counts
    pcounts = (counts + tm - 1) // tm * tm
    pend = jnp.cumsum(pcounts)
    dest = ((pend - pcounts)[s_eid] + jnp.arange(n_assign, dtype=jnp.int32) - start[s_eid]).astype(jnp.int32)
    n_blocks = -(-n_assign // tm) + N_EXPERTS
    tok_buf = jnp.zeros((n_blocks * tm,), jnp.int32).at[dest].set(s_tok)
    pos = jnp.zeros((n_assign,), jnp.int32).at[order].set(dest)
    blk_start = jnp.arange(n_blocks, dtype=jnp.int32) * tm
    blk_eid = jnp.minimum(jnp.sum(blk_start[:, None] >= pend[None, :], axis=1), N_EXPERTS - 1).astype(jnp.int32)
    return blk_eid, tok_buf.reshape(n_blocks, 1, tm), pos.reshape(n_tok, TOP_K_INNER)


def _layer(x, p, alpha):
    (w_in, b_in, lam_re, lam_im, log_dt, ssm_b_re, ssm_b_im, ssm_c_re, ssm_c_im,
     ssm_d, w_glu, b_glu, conv_w, conv_b, ln_c_g, ln_c_b, w_cout, b_cout, w_out, b_out,
     ln1_g, ln1_b, w_rg, b_rg, w_re, b_re, w_gate, w_up, w_down, ln2_g, ln2_b) = p
    bsz, seq, _ = x.shape
    n_tok = bsz * seq
    assert seq % CHUNK == 0
    nj = seq // CHUNK
    assert nj & (nj - 1) == 0, "chunks per sequence must be a power of two"
    n_steps = max(1, int(math.log2(nj)))
    assert n_steps <= 8

    n_ucv = D_SSM + 2 * D_CONV
    row = lambda a: a.reshape(1, -1).astype(F32)

    x2 = x.reshape(n_tok, D_MODEL)
    tm = min(512, n_tok)
    u, v = _inproj(x2, w_in[:, :n_ucv].astype(BF16), row(b_in[:n_ucv]), tm)

    mz, w_state, coef = _s5_chunk_weights(lam_re, lam_im, log_dt, ssm_b_re, ssm_b_im,
                                          ssm_c_re, ssm_c_im, n_steps)
    u_im = (u.astype(BF16).reshape(bsz * nj, CHUNK, N_SSM_GROUPS, SSM_GROUP)
            .transpose(2, 0, 1, 3).reshape(N_SSM_GROUPS, bsz * nj, CHUNK_F))
    y_im = _s5(u_im, mz, w_state, coef, nj, n_steps)
    y = (y_im.reshape(N_SSM_GROUPS, bsz * nj, CHUNK, SSM_GROUP)
         .transpose(1, 2, 0, 3).reshape(bsz, seq, D_SSM))

    w_r = jnp.zeros((D_MODEL, ROUTE_LANES), F32)
    w_r = w_r.at[:, :N_GROUPS].set(w_rg).at[:, N_GROUPS:N_GROUPS + N_EXPERTS].set(w_re)
    b_r = jnp.zeros((1, ROUTE_LANES), F32)
    b_r = b_r.at[0, :N_GROUPS].set(b_rg).at[0, N_GROUPS:N_GROUPS + N_EXPERTS].set(b_re)
    w_r_hi = w_r.astype(BF16)
    w_r_lo = (w_r - w_r_hi.astype(F32)).astype(BF16)
    wts = (w_in[:, n_ucv:].astype(BF16), row(b_in[n_ucv:]), row(ssm_d),
           w_glu.astype(BF16), row(b_glu),
           conv_w.reshape(CONV_WIDTH, D_CONV).astype(F32), row(conv_b), row(ln_c_g), row(ln_c_b),
           w_cout.astype(BF16), row(b_cout), w_out.astype(BF16), row(b_out),
           row(ln1_g), row(ln1_b), w_r_hi, w_r_lo, b_r)
    tl = min(512, seq)
    h3, route3 = _mix(x, u.reshape(bsz, seq, D_SSM), y, v.reshape(bsz, seq, D_CONV), wts, tl, alpha)
    h2 = h3.reshape(n_tok, D_MODEL)
    route2 = route3.reshape(n_tok, ROUTE_LANES)

    tm_moe = 256
    eid = route2[:, :TOP_K_INNER].astype(jnp.int32)
    blk_eid, tok_blocks, pos = _dispatch(eid, tm_moe)
    w_gu = jnp.concatenate([w_gate, w_up], axis=-1).astype(BF16)
    y_buf = _moe(blk_eid, tok_blocks, h2, w_gu, w_down.astype(BF16), tm_moe)
    tt = min(256, n_tok)
    pos_blocks = pos.reshape(n_tok // tt, tt, TOP_K_INNER).transpose(0, 2, 1)
    out = _combine(pos_blocks, h2, route2, y_buf, row(ln2_g), row(ln2_b), tt, alpha)
    return out.reshape(bsz, seq, D_MODEL)


def kernel(x, w_in, b_in, lam_re, lam_im, log_dt, ssm_b_re, ssm_b_im, ssm_c_re, ssm_c_im, ssm_d, w_glu, b_glu, conv_w, conv_b, ln_c_g, ln_c_b, w_cout, b_cout, w_out, b_out, ln1_g, ln1_b, w_route_group, b_route_group, w_route_expert, b_route_expert, w_gate, w_up, w_down, ln2_g, ln2_b):
    params = (w_in, b_in, lam_re, lam_im, log_dt, ssm_b_re, ssm_b_im, ssm_c_re, ssm_c_im,
              ssm_d, w_glu, b_glu, conv_w, conv_b, ln_c_g, ln_c_b, w_cout, b_cout, w_out, b_out,
              ln1_g, ln1_b, w_route_group, b_route_group, w_route_expert, b_route_expert,
              w_gate, w_up, w_down, ln2_g, ln2_b)
    depth = w_in.shape[0]
    alpha = (2.0 * depth) ** 0.25
    h = x
    for i in range(depth):
        h = _layer(h, tuple(p[i] for p in params), alpha)
    return h
```

```python
import functools
import math

import jax
import jax.numpy as jnp
from jax import lax
from jax.experimental import pallas as pl
from jax.experimental.pallas import tpu as pltpu

F32 = jnp.float32
BF16 = jnp.bfloat16
HIGHEST = lax.Precision.HIGHEST

D_MODEL = 1024
D_SSM = 512
SSM_GROUP = 16
N_SSM_GROUPS = D_SSM // SSM_GROUP
SSM_STATE = 64
D_CONV = 512
CONV_WIDTH = 31
N_GROUPS = 4
N_EXP_PER_GROUP = 8
N_EXPERTS = N_GROUPS * N_EXP_PER_GROUP
TOP_K_INNER = 2
D_FF_EXPERT = D_MODEL // 4
LN_EPS = 1e-5

CHUNK = 16
CHUNK_F = CHUNK * SSM_GROUP
STATE_F = 2 * SSM_STATE
ROUTE_LANES = 128
HALO = 32
SUBLANES = 8
VMEM_LIMIT = 56 * 1024 * 1024

_NT = (((1,), (1,)), ((), ()))
_TN = (((0,), (0,)), ((), ()))


def _layer_norm(x, g, b):
    mu = jnp.mean(x, axis=-1, keepdims=True)
    xc = x - mu
    var = jnp.mean(xc * xc, axis=-1, keepdims=True)
    return xc * lax.rsqrt(var + LN_EPS) * g + b


def _inproj_kernel(x_ref, wut_ref, but_ref, wcv_ref, bcv_ref, ut_ref, v_ref):
    x = x_ref[...].astype(BF16)
    ut = lax.dot_general(wut_ref[...], x, _NT, preferred_element_type=F32) + but_ref[...]
    ut_ref[...] = ut.astype(BF16)
    p = jnp.dot(x, wcv_ref[...], preferred_element_type=F32) + bcv_ref[...]
    v_ref[...] = p[:, :D_CONV] * jax.nn.sigmoid(p[:, D_CONV:])


def _inproj(x2, w_ut, b_ut, w_cv, b_cv, rg):
    n_rows = x2.shape[0]
    const = lambda i, t: (0, 0)
    return pl.pallas_call(
        _inproj_kernel,
        grid=(n_rows // rg, CHUNK),
        in_specs=[
            pl.BlockSpec((rg, D_MODEL), lambda i, t: (i, t)),
            pl.BlockSpec(w_ut.shape, const),
            pl.BlockSpec(b_ut.shape, const),
            pl.BlockSpec(w_cv.shape, const),
            pl.BlockSpec(b_cv.shape, const),
        ],
        out_specs=[
            pl.BlockSpec((None, D_SSM, rg), lambda i, t: (t, 0, i)),
            pl.BlockSpec((rg, D_CONV), lambda i, t: (i, t)),
        ],
        out_shape=[
            jax.ShapeDtypeStruct((CHUNK, D_SSM, n_rows), BF16),
            jax.ShapeDtypeStruct((n_rows, CHUNK * D_CONV), F32),
        ],
        compiler_params=pltpu.CompilerParams(
            dimension_semantics=("parallel", "arbitrary"), vmem_limit_bytes=VMEM_LIMIT),
        name="inproj",
    )(x2, w_ut, b_ut, w_cv, b_cv)


def _s5_chunk_weights(lam_re, lam_im, log_dt, b_re, b_im, c_re, c_im, n_steps):
    f32 = F32
    lr, li = lam_re.astype(f32), lam_im.astype(f32)
    dt = jnp.exp(log_dt.astype(f32))[:, None]
    mag = jnp.exp(lr * dt)
    ar, ai = mag * jnp.cos(li * dt), mag * jnp.sin(li * dt)
    den = lr * lr + li * li
    fr = ((ar - 1.0) * lr + ai * li) / den
    fi = (ai * lr - (ar - 1.0) * li) / den
    br, bi = b_re.astype(f32), b_im.astype(f32)
    bbr = fr[..., None] * br - fi[..., None] * bi
    bbi = fr[..., None] * bi + fi[..., None] * br
    cr, ci = c_re.astype(f32), c_im.astype(f32)

    pr, pi = [jnp.ones_like(ar)], [jnp.zeros_like(ai)]
    for _ in range(CHUNK):
        pr_n = pr[-1] * ar - pi[-1] * ai
        pi_n = pr[-1] * ai + pi[-1] * ar
        pr.append(pr_n)
        pi.append(pi_n)
    pw_r, pw_i = jnp.stack(pr), jnp.stack(pi)

    ca_r = cr[None] * pw_r[:, :, None, :] - ci[None] * pw_i[:, :, None, :]
    ca_i = cr[None] * pw_i[:, :, None, :] + ci[None] * pw_r[:, :, None, :]
    kk = (jnp.einsum('kghp,gpq->kghq', ca_r[:CHUNK], bbr, precision=HIGHEST)
          - jnp.einsum('kghp,gpq->kghq', ca_i[:CHUNK], bbi, precision=HIGHEST))
    t_in = jnp.arange(CHUNK)[:, None]
    t_out = jnp.arange(CHUNK)[None, :]
    lag = t_out - t_in
    m = kk[jnp.maximum(lag, 0)]
    m = jnp.where((lag >= 0)[:, :, None, None, None], m, 0.0)
    m = m.transpose(2, 1, 3, 0, 4).reshape(N_SSM_GROUPS, CHUNK_F, CHUNK_F)

    rp_r, rp_i = pw_r[CHUNK - 1::-1][:CHUNK], pw_i[CHUNK - 1::-1][:CHUNK]
    z_r = rp_r[..., None] * bbr[None] - rp_i[..., None] * bbi[None]
    z_i = rp_r[..., None] * bbi[None] + rp_i[..., None] * bbr[None]
    z = jnp.concatenate([z_r, z_i], axis=2)
    z = z.transpose(1, 2, 0, 3).reshape(N_SSM_GROUPS, STATE_F, CHUNK_F)
    mzt = jnp.concatenate([m, z], axis=1)

    w = jnp.concatenate([ca_r[1:], -ca_i[1:]], axis=-1)
    wt = w.transpose(1, 0, 2, 3).reshape(N_SSM_GROUPS, CHUNK_F, STATE_F)

    sr, si = pw_r[CHUNK], pw_i[CHUNK]
    c1, c2 = [], []
    for _ in range(n_steps):
        c1.append(jnp.concatenate([sr, sr], axis=-1)[None])
        c2.append(jnp.concatenate([-si, si], axis=-1)[None])
        sr, si = sr * sr - si * si, 2.0 * sr * si
    pad = jnp.zeros((8 - n_steps, N_SSM_GROUPS, STATE_F), f32)
    coef = jnp.concatenate(c1 + [pad] + c2 + [pad], axis=0).transpose(1, 2, 0)
    return mzt.astype(BF16), wt.astype(BF16), coef


def _s5_kernel(ut_ref, mzt_ref, wt_ref, coef_ref, dcol_ref, o_ref, *, nj, n_steps):
    ut = jnp.concatenate([ut_ref[t] for t in range(CHUNK)], axis=0)
    r = jnp.dot(mzt_ref[0], ut, preferred_element_type=F32)
    y0 = r[:CHUNK_F]
    t = r[CHUNK_F:]
    j = lax.broadcasted_iota(jnp.int32, t.shape, 1) & (nj - 1)
    for m in range(n_steps):
        d = 1 << m
        sh = jnp.where(j >= d, pltpu.roll(t, d, axis=1), 0.0)
        p1 = coef_ref[0, :, m:m + 1]
        p2 = coef_ref[0, :, 8 + m:9 + m]
        t = t + p1 * sh + p2 * pltpu.roll(sh, SSM_STATE, axis=0)
    s_prev = jnp.where(j >= 1, pltpu.roll(t, 1, axis=1), 0.0)
    y = y0 + jnp.dot(wt_ref[0], s_prev.astype(BF16), preferred_element_type=F32)
    ys = jax.nn.gelu(y + dcol_ref[0] * ut.astype(F32)).astype(BF16)
    for t_out in range(CHUNK):
        o_ref[t_out] = ys[t_out * SSM_GROUP:(t_out + 1) * SSM_GROUP]


def _s5(ut4, mzt, wt, coef, dcol, nj, n_steps):
    _, g, _, n_rows = ut4.shape
    blk = pl.BlockSpec((CHUNK, None, SSM_GROUP, n_rows), lambda i: (0, i, 0, 0))
    return pl.pallas_call(
        functools.partial(_s5_kernel, nj=nj, n_steps=n_steps),
        grid=(g,),
        in_specs=[
            blk,
            pl.BlockSpec((1, CHUNK_F + STATE_F, CHUNK_F), lambda i: (i, 0, 0)),
            pl.BlockSpec((1, CHUNK_F, STATE_F), lambda i: (i, 0, 0)),
            pl.BlockSpec((1, STATE_F, 16), lambda i: (i, 0, 0)),
            pl.BlockSpec((1, CHUNK_F, 1), lambda i: (i, 0, 0)),
        ],
        out_specs=blk,
        out_shape=jax.ShapeDtypeStruct(ut4.shape, BF16),
        compiler_params=pltpu.CompilerParams(
            dimension_semantics=("parallel",), vmem_limit_bytes=VMEM_LIMIT),
        name="s5",
    )(ut4, mzt, wt, coef, dcol)


def _conv_kernel(v_ref, vh_ref, cw_ref, cb_ref, g_ref, b_ref, c_ref, vext_ref, shift_ref, conv_ref, *, tl, sub):
    i = pl.program_id(1)
    n_ext = tl + HALO
    vext_ref[0:HALO, :] = jnp.where(i > 0, vh_ref[...], 0.0)
    vext_ref[HALO:, :] = v_ref[...]
    for m in range(SUBLANES):
        shift_ref[m, 0:n_ext - m, :] = vext_ref[m:n_ext, :]
    off = HALO - (CONV_WIDTH - 1)
    for s in range(tl // sub):
        acc = jnp.zeros((sub, D_CONV), F32) + cb_ref[...]
        for k in range(CONV_WIDTH):
            q, m = divmod(off + k, SUBLANES)
            r0 = s * sub + q * SUBLANES
            acc = acc + cw_ref[k:k + 1, :] * shift_ref[m, r0:r0 + sub, :]
        conv_ref[s * sub:(s + 1) * sub, :] = acc
    c = _layer_norm(conv_ref[...], g_ref[...], b_ref[...])
    c_ref[...] = jax.nn.silu(c).astype(BF16)


def _conv(v3, conv_w, conv_b, ln_g, ln_b, tl):
    b, l, _ = v3.shape
    sub = min(64, tl)
    hb = tl // HALO
    const = lambda bi, i: (0, 0)
    row_spec = pl.BlockSpec((None, tl, D_CONV), lambda bi, i: (bi, i, 0))
    halo_spec = pl.BlockSpec((None, HALO, D_CONV), lambda bi, i: (bi, jnp.maximum(i * hb - 1, 0), 0))
    return pl.pallas_call(
        functools.partial(_conv_kernel, tl=tl, sub=sub),
        grid=(b, l // tl),
        in_specs=[row_spec, halo_spec,
                  pl.BlockSpec(conv_w.shape, const), pl.BlockSpec(conv_b.shape, const),
                  pl.BlockSpec(ln_g.shape, const), pl.BlockSpec(ln_b.shape, const)],
        out_specs=row_spec,
        out_shape=jax.ShapeDtypeStruct((b, l, D_CONV), BF16),
        scratch_shapes=[
            pltpu.VMEM((tl + HALO, D_CONV), F32),
            pltpu.VMEM((SUBLANES, tl + HALO, D_CONV), F32),
            pltpu.VMEM((tl, D_CONV), F32),
        ],
        compiler_params=pltpu.CompilerParams(
            dimension_semantics=("parallel", "arbitrary"), vmem_limit_bytes=VMEM_LIMIT),
        name="conv",
    )(v3, v3, conv_w, conv_b, ln_g, ln_b)


def _mix_kernel(x_ref, yst_ref, c_ref,
                wg_ref, bg_ref, wglu_ref, bglu_ref, wco_ref, bco_ref,
                wo_ref, bo_ref, l1g_ref, l1b_ref, wrh_ref, wrl_ref, br_ref,
                h_ref, route_ref, *, alpha):
    c_out = jnp.dot(c_ref[...], wco_ref[...], preferred_element_type=F32) + bco_ref[...]

    glu = lax.dot_general(yst_ref[...], wglu_ref[...], _TN, preferred_element_type=F32) + bglu_ref[...]
    s_out = glu[:, :D_MODEL] * jax.nn.sigmoid(glu[:, D_MODEL:])

    x = x_ref[...]
    gates = jnp.dot(x.astype(BF16), wg_ref[...], preferred_element_type=F32) + bg_ref[...]
    merged = jax.nn.sigmoid(gates[:, :D_MODEL]) * s_out + jax.nn.sigmoid(gates[:, D_MODEL:]) * c_out
    mix = jnp.dot(merged.astype(BF16), wo_ref[...], preferred_element_type=F32) + bo_ref[...]
    h = _layer_norm(alpha * x + mix, l1g_ref[...], l1b_ref[...])
    h_ref[...] = h

    h_hi = h.astype(BF16)
    h_lo = (h - h_hi.astype(F32)).astype(BF16)
    z = (jnp.dot(h_hi, wrh_ref[...], preferred_element_type=F32)
         + jnp.dot(h_lo, wrh_ref[...], preferred_element_type=F32)
         + jnp.dot(h_hi, wrl_ref[...], preferred_element_type=F32)) + br_ref[...]
    lane = lax.broadcasted_iota(jnp.int32, z.shape, 1)
    lanef = lane.astype(F32)
    neg = jnp.float32(-jnp.inf)
    big = jnp.float32(1e9)
    gmask = lane < N_GROUPS
    zg = jnp.where(gmask, z, neg)
    gmax = jnp.max(zg, axis=1, keepdims=True)
    gsel = jnp.min(jnp.where(gmask & (zg == gmax), lanef, big), axis=1, keepdims=True)
    p_group = 1.0 / jnp.sum(jnp.where(gmask, jnp.exp(zg - gmax), 0.0), axis=1, keepdims=True)
    lo = N_GROUPS + N_EXP_PER_GROUP * gsel
    emask = (lanef >= lo) & (lanef < lo + N_EXP_PER_GROUP)
    ze = jnp.where(emask, z, neg)
    v1 = jnp.max(ze, axis=1, keepdims=True)
    i1 = jnp.min(jnp.where(emask & (ze == v1), lanef, big), axis=1, keepdims=True)
    emask2 = emask & (lanef != i1)
    ze2 = jnp.where(emask2, z, neg)
    v2 = jnp.max(ze2, axis=1, keepdims=True)
    i2 = jnp.min(jnp.where(emask2 & (ze2 == v2), lanef, big), axis=1, keepdims=True)
    e2 = jnp.exp(v2 - v1)
    inv = 1.0 / (1.0 + e2)
    g1 = p_group * inv
    g2 = p_group * (e2 * inv)
    out = jnp.where(lane == 0, i1 - N_GROUPS, 0.0)
    out = jnp.where(lane == 1, i2 - N_GROUPS, out)
    out = jnp.where(lane == 2, g1, out)
    out = jnp.where(lane == 3, g2, out)
    route_ref[...] = out


def _mix(x2, yst, c2, wts, rg, alpha):
    n_rows = x2.shape[0]

    def tau_spec(f):
        return pl.BlockSpec((rg, f), lambda i, t: (i, t))

    def full_spec(a):
        nd = a.ndim
        return pl.BlockSpec(a.shape, lambda i, t: (0,) * nd)

    return pl.pallas_call(
        functools.partial(_mix_kernel, alpha=alpha),
        grid=(n_rows // rg, CHUNK),
        in_specs=[tau_spec(D_MODEL),
                  pl.BlockSpec((None, D_SSM, rg), lambda i, t: (t, 0, i)),
                  tau_spec(D_CONV)]
                 + [full_spec(a) for a in wts],
        out_specs=[tau_spec(D_MODEL), tau_spec(ROUTE_LANES)],
        out_shape=[
            jax.ShapeDtypeStruct((n_rows, CHUNK * D_MODEL), F32),
            jax.ShapeDtypeStruct((n_rows, CHUNK * ROUTE_LANES), F32),
        ],
        compiler_params=pltpu.CompilerParams(
            dimension_semantics=("parallel", "arbitrary"), vmem_limit_bytes=VMEM_LIMIT),
        name="mix",
    )(x2, yst, c2, *wts)


def _gather_rows(idx_ref, k, src_hbm, dst, sem, n):
    def issue(r, carry):
        pltpu.make_async_copy(src_hbm.at[pl.ds(idx_ref[0, k, r], 1)], dst.at[pl.ds(r, 1)], sem).start()
        return carry
    lax.fori_loop(0, n, issue, 0, unroll=8)


def _wait_rows(src_hbm, dst, sem, n):
    pltpu.make_async_copy(src_hbm.at[pl.ds(0, n)], dst, sem).wait()


def _moe_kernel(eid_ref, tok_ref, tok_next_ref, h_hbm, wgu_ref, wd_ref, y_ref, xbuf, sem, *, tm):
    del eid_ref
    i = pl.program_id(0)
    n = pl.num_programs(0)
    slot = i % 2

    @pl.when(i == 0)
    def _():
        _gather_rows(tok_ref, 0, h_hbm, xbuf.at[0], sem.at[0], tm)

    @pl.when(i + 1 < n)
    def _():
        _gather_rows(tok_next_ref, 0, h_hbm, xbuf.at[1 - slot], sem.at[1 - slot], tm)

    _wait_rows(h_hbm, xbuf.at[slot], sem.at[slot], tm)
    xb = xbuf[slot].astype(BF16)
    gu = jnp.dot(xb, wgu_ref[0], preferred_element_type=F32)
    hid = jax.nn.silu(gu[:, :D_FF_EXPERT]) * gu[:, D_FF_EXPERT:]
    y_ref[...] = jnp.dot(hid.astype(BF16), wd_ref[0], preferred_element_type=F32)


def _moe(blk_eid, tok_blocks, h2, w_gu, w_d, tm):
    n_blocks = tok_blocks.shape[0]
    grid_spec = pltpu.PrefetchScalarGridSpec(
        num_scalar_prefetch=1,
        grid=(n_blocks,),
        in_specs=[
            pl.BlockSpec((1, 1, tm), lambda i, eid: (i, 0, 0), memory_space=pltpu.SMEM),
            pl.BlockSpec((1, 1, tm), lambda i, eid: (jnp.minimum(i + 1, n_blocks - 1), 0, 0),
                         memory_space=pltpu.SMEM),
            pl.BlockSpec(memory_space=pl.ANY),
            pl.BlockSpec((1, D_MODEL, 2 * D_FF_EXPERT), lambda i, eid: (eid[i], 0, 0)),
            pl.BlockSpec((1, D_FF_EXPERT, D_MODEL), lambda i, eid: (eid[i], 0, 0)),
        ],
        out_specs=pl.BlockSpec((tm, D_MODEL), lambda i, eid: (i, 0)),
        scratch_shapes=[pltpu.VMEM((2, tm, D_MODEL), F32), pltpu.SemaphoreType.DMA((2,))],
    )
    return pl.pallas_call(
        functools.partial(_moe_kernel, tm=tm),
        grid_spec=grid_spec,
        out_shape=jax.ShapeDtypeStruct((n_blocks * tm, D_MODEL), F32),
        compiler_params=pltpu.CompilerParams(
            dimension_semantics=("arbitrary",), vmem_limit_bytes=VMEM_LIMIT),
        name="moe",
    )(blk_eid, tok_blocks, tok_blocks, h2, w_gu, w_d)


def _combine_kernel(pos_ref, pos_next_ref, h_ref, route_ref, y_hbm, g_ref, b_ref, o_ref, buf, sem, *, tt, alpha):
    i = pl.program_id(0)
    n = pl.num_programs(0)
    slot = i % 2

    def gather(idx_ref, s):
        for k in range(TOP_K_INNER):
            _gather_rows(idx_ref, k, y_hbm, buf.at[s, k], sem.at[s, k], tt)

    @pl.when(i == 0)
    def _():
        gather(pos_ref, 0)

    @pl.when(i + 1 < n)
    def _():
        gather(pos_next_ref, 1 - slot)

    for k in range(TOP_K_INNER):
        _wait_rows(y_hbm, buf.at[slot, k], sem.at[slot, k], tt)
    w0 = route_ref[:, 2:3]
    w1 = route_ref[:, 3:4]
    moe = w0 * buf[slot, 0] + w1 * buf[slot, 1]
    o_ref[...] = _layer_norm(alpha * h_ref[...] + moe, g_ref[...], b_ref[...])


def _combine(pos_blocks, h2, route2, y_buf, ln_g, ln_b, tt, alpha):
    n_tok = h2.shape[0]
    n_tiles = n_tok // tt
    return pl.pallas_call(
        functools.partial(_combine_kernel, tt=tt, alpha=alpha),
        grid=(n_tiles,),
        in_specs=[
            pl.BlockSpec((1, TOP_K_INNER, tt), lambda i: (i, 0, 0), memory_space=pltpu.SMEM),
            pl.BlockSpec((1, TOP_K_INNER, tt), lambda i: (jnp.minimum(i + 1, n_tiles - 1), 0, 0),
                         memory_space=pltpu.SMEM),
            pl.BlockSpec((tt, D_MODEL), lambda i: (i, 0)),
            pl.BlockSpec((tt, ROUTE_LANES), lambda i: (i, 0)),
            pl.BlockSpec(memory_space=pl.ANY),
            pl.BlockSpec((1, D_MODEL), lambda i: (0, 0)),
            pl.BlockSpec((1, D_MODEL), lambda i: (0, 0)),
        ],
        out_specs=pl.BlockSpec((tt, D_MODEL), lambda i: (i, 0)),
        out_shape=jax.ShapeDtypeStruct((n_tok, D_MODEL), F32),
        scratch_shapes=[
            pltpu.VMEM((2, TOP_K_INNER, tt, D_MODEL), F32),
            pltpu.SemaphoreType.DMA((2, TOP_K_INNER)),
        ],
        compiler_params=pltpu.CompilerParams(
            dimension_semantics=("arbitrary",), vmem_limit_bytes=VMEM_LIMIT),
        name="combine",
    )(pos_blocks, pos_blocks, h2, route2, y_buf, ln_g, ln_b)


def _dispatch(eid, tm):
    n_tok = eid.shape[0]
    n_assign = n_tok * TOP_K_INNER
    i32 = jnp.int32
    eflat = eid.reshape(-1)
    order = jnp.argsort(eflat).astype(i32)
    inv = jnp.argsort(order).astype(i32)
    counts = jnp.sum(eflat[:, None] == jnp.arange(N_EXPERTS, dtype=i32)[None, :], axis=0).astype(i32)
    start = jnp.cumsum(counts) - counts
    pcounts = (counts + tm - 1) // tm * tm
    pend = jnp.cumsum(pcounts)
    pstart = pend - pcounts
    n_blocks = -(-n_assign // tm) + N_EXPERTS
    blk_start = jnp.arange(n_blocks, dtype=i32) * tm
    blk_eid = jnp.minimum(jnp.sum(blk_start[:, None] >= pend[None, :], axis=1), N_EXPERTS - 1).astype(i32)
    row = jnp.arange(n_blocks * tm, dtype=i32)
    row_e = jnp.repeat(blk_eid, tm)
    within = row - pstart[row_e]
    valid = within < counts[row_e]
    src = jnp.clip(start[row_e] + within, 0, n_assign - 1)
    tok_buf = jnp.where(valid, order[src] // TOP_K_INNER, 0).astype(i32)
    pos = (pstart[eflat] + inv - start[eflat]).astype(i32)
    return blk_eid, tok_buf.reshape(n_blocks, 1, tm), pos.reshape(n_tok, TOP_K_INNER)


def _layer(x, p, alpha):
    (w_in, b_in, lam_re, lam_im, log_dt, ssm_b_re, ssm_b_im, ssm_c_re, ssm_c_im,
     ssm_d, w_glu, b_glu, conv_w, conv_b, ln_c_g, ln_c_b, w_cout, b_cout, w_out, b_out,
     ln1_g, ln1_b, w_rg, b_rg, w_re, b_re, w_gate, w_up, w_down, ln2_g, ln2_b) = p
    bsz, seq, _ = x.shape
    n_tok = bsz * seq
    assert seq % CHUNK == 0
    nj = seq // CHUNK
    n_rows = bsz * nj
    assert nj & (nj - 1) == 0, "chunks per sequence must be a power of two"
    n_steps = max(1, int(math.log2(nj)))
    assert n_steps <= 8

    n_ucv = D_SSM + 2 * D_CONV
    row = lambda a: a.reshape(1, -1).astype(F32)
    rg = min(512, n_rows)

    x2 = x.reshape(n_rows, CHUNK * D_MODEL)
    ut, v2 = _inproj(x2, w_in[:, :D_SSM].T.astype(BF16), b_in[:D_SSM].reshape(-1, 1).astype(F32),
                     w_in[:, D_SSM:n_ucv].astype(BF16), row(b_in[D_SSM:n_ucv]), rg)

    mzt, wt_state, coef = _s5_chunk_weights(lam_re, lam_im, log_dt, ssm_b_re, ssm_b_im,
                                            ssm_c_re, ssm_c_im, n_steps)
    dcol = jnp.tile(ssm_d.astype(F32).reshape(N_SSM_GROUPS, 1, SSM_GROUP), (1, CHUNK, 1))
    dcol = dcol.reshape(N_SSM_GROUPS, CHUNK_F, 1)
    yst = _s5(ut.reshape(CHUNK, N_SSM_GROUPS, SSM_GROUP, n_rows), mzt, wt_state, coef, dcol, nj, n_steps)
    yst = yst.reshape(CHUNK, D_SSM, n_rows)

    tl = min(512, seq)
    c_act = _conv(v2.reshape(bsz, seq, D_CONV), conv_w.reshape(CONV_WIDTH, D_CONV).astype(F32),
                  row(conv_b), row(ln_c_g), row(ln_c_b), tl)

    w_r = jnp.zeros((D_MODEL, ROUTE_LANES), F32)
    w_r = w_r.at[:, :N_GROUPS].set(w_rg).at[:, N_GROUPS:N_GROUPS + N_EXPERTS].set(w_re)
    b_r = jnp.zeros((1, ROUTE_LANES), F32)
    b_r = b_r.at[0, :N_GROUPS].set(b_rg).at[0, N_GROUPS:N_GROUPS + N_EXPERTS].set(b_re)
    w_r_hi = w_r.astype(BF16)
    w_r_lo = (w_r - w_r_hi.astype(F32)).astype(BF16)
    wts = (w_in[:, n_ucv:].astype(BF16), row(b_in[n_ucv:]),
           w_glu.astype(BF16), row(b_glu),
           w_cout.astype(BF16), row(b_cout), w_out.astype(BF16), row(b_out),
           row(ln1_g), row(ln1_b), w_r_hi, w_r_lo, b_r)
    h2, route2 = _mix(x2, yst, c_act.reshape(n_rows, CHUNK * D_CONV), wts, rg, alpha)
    h2 = h2.reshape(n_tok, D_MODEL)
    route2 = route2.reshape(n_tok, ROUTE_LANES)

    tm_moe = 256
    eid = route2[:, :TOP_K_INNER].astype(jnp.int32)
    blk_eid, tok_blocks, pos = _dispatch(eid, tm_moe)
    w_gu = jnp.concatenate([w_gate, w_up], axis=-1).astype(BF16)
    y_buf = _moe(blk_eid, tok_blocks, h2, w_gu, w_down.astype(BF16), tm_moe)
    tt = min(256, n_tok)
    pos_blocks = pos.reshape(n_tok // tt, tt, TOP_K_INNER).transpose(0, 2, 1)
    out = _combine(pos_blocks, h2, route2, y_buf, row(ln2_g), row(ln2_b), tt, alpha)
    return out.reshape(bsz, seq, D_MODEL)


def kernel(x, w_in, b_in, lam_re, lam_im, log_dt, ssm_b_re, ssm_b_im, ssm_c_re, ssm_c_im, ssm_d, w_glu, b_glu, conv_w, conv_b, ln_c_g, ln_c_b, w_cout, b_cout, w_out, b_out, ln1_g, ln1_b, w_route_group, b_route_group, w_route_expert, b_route_expert, w_gate, w_up, w_down, ln2_g, ln2_b):
    params = (w_in, b_in, lam_re, lam_im, log_dt, ssm_b_re, ssm_b_im, ssm_c_re, ssm_c_im,
              ssm_d, w_glu, b_glu, conv_w, conv_b, ln_c_g, ln_c_b, w_cout, b_cout, w_out, b_out,
              ln1_g, ln1_b, w_route_group, b_route_group, w_route_expert, b_route_expert,
              w_gate, w_up, w_down, ln2_g, ln2_b)
    depth = w_in.shape[0]
    alpha = (2.0 * depth) ** 0.25
    h = x
    for i in range(depth):
        h = _layer(h, tuple(p[i] for p in params), alpha)
    return h
```

```python
import functools
import math

import jax
import jax.numpy as jnp
from jax import lax
from jax.experimental import pallas as pl
from jax.experimental.pallas import tpu as pltpu

F32 = jnp.float32
BF16 = jnp.bfloat16
HIGHEST = lax.Precision.HIGHEST

D_MODEL = 1024
D_SSM = 512
SSM_GROUP = 16
N_SSM_GROUPS = D_SSM // SSM_GROUP
SSM_STATE = 64
D_CONV = 512
CONV_WIDTH = 31
N_GROUPS = 4
N_EXP_PER_GROUP = 8
N_EXPERTS = N_GROUPS * N_EXP_PER_GROUP
TOP_K_INNER = 2
N_PAIRS = N_EXP_PER_GROUP * (N_EXP_PER_GROUP - 1) // 2
N_CLASSES = N_GROUPS * N_PAIRS
D_FF_EXPERT = D_MODEL // 4
LN_EPS = 1e-5

CHUNK = 16
CHUNK_F = CHUNK * SSM_GROUP
STATE_F = 2 * SSM_STATE
ROUTE_LANES = 128
HALO = 32
HALO_CHUNKS = 8
SUBLANES = 8
LANES = 128
VMEM_LIMIT = 56 * 1024 * 1024

_NT = (((1,), (1,)), ((), ()))
_TN = (((0,), (0,)), ((), ()))


def _layer_norm(x, g, b):
    mu = jnp.mean(x, axis=-1, keepdims=True)
    xc = x - mu
    var = jnp.mean(xc * xc, axis=-1, keepdims=True)
    return xc * lax.rsqrt(var + LN_EPS) * g + b


def _inproj_kernel(x_ref, wut_ref, but_ref, wcv_ref, bcv_ref, ut_ref, v_ref):
    x = x_ref[...].astype(BF16)
    ut = lax.dot_general(wut_ref[...], x, _NT, preferred_element_type=F32) + but_ref[...]
    ut_ref[...] = ut.astype(BF16)
    p = jnp.dot(x, wcv_ref[...], preferred_element_type=F32) + bcv_ref[...]
    v_ref[...] = p[:, :D_CONV] * jax.nn.sigmoid(p[:, D_CONV:])


def _inproj(x2, w_ut, b_ut, w_cv, b_cv, rg):
    n_rows = x2.shape[0]
    const = lambda i, t: (0, 0)
    return pl.pallas_call(
        _inproj_kernel,
        grid=(n_rows // rg, CHUNK),
        in_specs=[
            pl.BlockSpec((rg, D_MODEL), lambda i, t: (i, t)),
            pl.BlockSpec(w_ut.shape, const),
            pl.BlockSpec(b_ut.shape, const),
            pl.BlockSpec(w_cv.shape, const),
            pl.BlockSpec(b_cv.shape, const),
        ],
        out_specs=[
            pl.BlockSpec((None, D_SSM, rg), lambda i, t: (t, 0, i)),
            pl.BlockSpec((None, rg, D_CONV), lambda i, t: (t, i, 0)),
        ],
        out_shape=[
            jax.ShapeDtypeStruct((CHUNK, D_SSM, n_rows), BF16),
            jax.ShapeDtypeStruct((CHUNK, n_rows, D_CONV), F32),
        ],
        compiler_params=pltpu.CompilerParams(
            dimension_semantics=("parallel", "arbitrary"), vmem_limit_bytes=VMEM_LIMIT),
        name="inproj",
    )(x2, w_ut, b_ut, w_cv, b_cv)


def _s5_chunk_weights(lam_re, lam_im, log_dt, b_re, b_im, c_re, c_im, n_steps):
    f32 = F32
    lr, li = lam_re.astype(f32), lam_im.astype(f32)
    dt = jnp.exp(log_dt.astype(f32))[:, None]
    mag = jnp.exp(lr * dt)
    ar, ai = mag * jnp.cos(li * dt), mag * jnp.sin(li * dt)
    den = lr * lr + li * li
    fr = ((ar - 1.0) * lr + ai * li) / den
    fi = (ai * lr - (ar - 1.0) * li) / den
    br, bi = b_re.astype(f32), b_im.astype(f32)
    bbr = fr[..., None] * br - fi[..., None] * bi
    bbi = fr[..., None] * bi + fi[..., None] * br
    cr, ci = c_re.astype(f32), c_im.astype(f32)

    pr, pi = [jnp.ones_like(ar)], [jnp.zeros_like(ai)]
    for _ in range(CHUNK):
        pr_n = pr[-1] * ar - pi[-1] * ai
        pi_n = pr[-1] * ai + pi[-1] * ar
        pr.append(pr_n)
        pi.append(pi_n)
    pw_r, pw_i = jnp.stack(pr), jnp.stack(pi)

    ca_r = cr[None] * pw_r[:, :, None, :] - ci[None] * pw_i[:, :, None, :]
    ca_i = cr[None] * pw_i[:, :, None, :] + ci[None] * pw_r[:, :, None, :]
    kk = (jnp.einsum('kghp,gpq->kghq', ca_r[:CHUNK], bbr, precision=HIGHEST)
          - jnp.einsum('kghp,gpq->kghq', ca_i[:CHUNK], bbi, precision=HIGHEST))
    t_in = jnp.arange(CHUNK)[:, None]
    t_out = jnp.arange(CHUNK)[None, :]
    lag = t_out - t_in
    m = kk[jnp.maximum(lag, 0)]
    m = jnp.where((lag >= 0)[:, :, None, None, None], m, 0.0)
    m = m.transpose(2, 1, 3, 0, 4).reshape(N_SSM_GROUPS, CHUNK_F, CHUNK_F)

    rp_r, rp_i = pw_r[CHUNK - 1::-1][:CHUNK], pw_i[CHUNK - 1::-1][:CHUNK]
    z_r = rp_r[..., None] * bbr[None] - rp_i[..., None] * bbi[None]
    z_i = rp_r[..., None] * bbi[None] + rp_i[..., None] * bbr[None]
    z = jnp.concatenate([z_r, z_i], axis=2)
    z = z.transpose(1, 2, 0, 3).reshape(N_SSM_GROUPS, STATE_F, CHUNK_F)
    mzt = jnp.concatenate([m, z], axis=1)

    w = jnp.concatenate([ca_r[1:], -ca_i[1:]], axis=-1)
    wt = w.transpose(1, 0, 2, 3).reshape(N_SSM_GROUPS, CHUNK_F, STATE_F)

    sr, si = pw_r[CHUNK], pw_i[CHUNK]
    c1, c2 = [], []
    for _ in range(n_steps):
        c1.append(jnp.concatenate([sr, sr], axis=-1)[None])
        c2.append(jnp.concatenate([-si, si], axis=-1)[None])
        sr, si = sr * sr - si * si, 2.0 * sr * si
    pad = jnp.zeros((8 - n_steps, N_SSM_GROUPS, STATE_F), f32)
    coef = jnp.concatenate(c1 + [pad] + c2 + [pad], axis=0).transpose(1, 2, 0)
    return mzt.astype(BF16), wt.astype(BF16), coef


def _s5_kernel(ut_ref, mzt_ref, wt_ref, coef_ref, dcol_ref, o_ref, *, nj, n_steps):
    ut = jnp.concatenate([ut_ref[t] for t in range(CHUNK)], axis=0)
    r = jnp.dot(mzt_ref[0], ut, preferred_element_type=F32)
    y0 = r[:CHUNK_F]
    t = r[CHUNK_F:]
    j = lax.broadcasted_iota(jnp.int32, t.shape, 1) & (nj - 1)
    for m in range(n_steps):
        d = 1 << m
        sh = jnp.where(j >= d, pltpu.roll(t, d, axis=1), 0.0)
        p1 = coef_ref[0, :, m:m + 1]
        p2 = coef_ref[0, :, 8 + m:9 + m]
        t = t + p1 * sh + p2 * pltpu.roll(sh, SSM_STATE, axis=0)
    s_prev = jnp.where(j >= 1, pltpu.roll(t, 1, axis=1), 0.0)
    y = y0 + jnp.dot(wt_ref[0], s_prev.astype(BF16), preferred_element_type=F32)
    ys = jax.nn.gelu(y + dcol_ref[0] * ut.astype(F32)).astype(BF16)
    for t_out in range(CHUNK):
        o_ref[t_out] = ys[t_out * SSM_GROUP:(t_out + 1) * SSM_GROUP]


def _s5(ut4, mzt, wt, coef, dcol, nj, n_steps):
    _, g, _, n_rows = ut4.shape
    blk = pl.BlockSpec((CHUNK, None, SSM_GROUP, n_rows), lambda i: (0, i, 0, 0))
    return pl.pallas_call(
        functools.partial(_s5_kernel, nj=nj, n_steps=n_steps),
        grid=(g,),
        in_specs=[
            blk,
            pl.BlockSpec((1, CHUNK_F + STATE_F, CHUNK_F), lambda i: (i, 0, 0)),
            pl.BlockSpec((1, CHUNK_F, STATE_F), lambda i: (i, 0, 0)),
            pl.BlockSpec((1, STATE_F, 16), lambda i: (i, 0, 0)),
            pl.BlockSpec((1, CHUNK_F, 1), lambda i: (i, 0, 0)),
        ],
        out_specs=blk,
        out_shape=jax.ShapeDtypeStruct(ut4.shape, BF16),
        compiler_params=pltpu.CompilerParams(
            dimension_semantics=("parallel",), vmem_limit_bytes=VMEM_LIMIT),
        name="s5",
    )(ut4, mzt, wt, coef, dcol)


def _conv_kernel(v_ref, vh_ref, cw_ref, cb_ref, g_ref, b_ref, c_ref, vext_ref, shift_ref, conv_ref, *, tl, sub):
    i = pl.program_id(1)
    n_ext = tl + HALO
    n_chunks = tl // CHUNK
    n_hist = HALO // CHUNK
    blocks = [slice(c * LANES, (c + 1) * LANES) for c in range(D_CONV // LANES)]
    for t in range(CHUNK):
        for c, lanes in enumerate(blocks):
            hist = vh_ref[t, HALO_CHUNKS - n_hist:HALO_CHUNKS, lanes]
            vext_ref[c, pl.ds(t, n_hist, stride=CHUNK), :] = jnp.where(i > 0, hist, 0.0)
            vext_ref[c, pl.ds(HALO + t, n_chunks, stride=CHUNK), :] = v_ref[t, :, lanes]
    for m in range(SUBLANES):
        shift_ref[m, :, 0:n_ext - m, :] = vext_ref[:, m:n_ext, :]
    off = HALO - (CONV_WIDTH - 1)
    for c, lanes in enumerate(blocks):
        for s in range(tl // sub):
            acc = jnp.zeros((sub, LANES), F32) + cb_ref[:, lanes]
            for k in range(CONV_WIDTH):
                q, m = divmod(off + k, SUBLANES)
                r0 = s * sub + q * SUBLANES
                acc = acc + cw_ref[k:k + 1, lanes] * shift_ref[m, c, r0:r0 + sub, :]
            conv_ref[c, s * sub:(s + 1) * sub, :] = acc
    xs = [conv_ref[c] for c in range(len(blocks))]
    mu = sum(jnp.sum(x, axis=-1, keepdims=True) for x in xs) * (1.0 / D_CONV)
    xc = [x - mu for x in xs]
    var = sum(jnp.sum(x * x, axis=-1, keepdims=True) for x in xc) * (1.0 / D_CONV)
    inv = lax.rsqrt(var + LN_EPS)
    for c, lanes in enumerate(blocks):
        conv_ref[c] = jax.nn.silu(xc[c] * inv * g_ref[:, lanes] + b_ref[:, lanes])
    for t in range(CHUNK):
        for c, lanes in enumerate(blocks):
            c_ref[t, :, lanes] = conv_ref[c, pl.ds(t, n_chunks, stride=CHUNK), :].astype(BF16)


def _conv(vt, b, l, conv_w, conv_b, ln_g, ln_b, tl):
    sub = min(64, tl)
    n_chunks = tl // CHUNK
    assert n_chunks % HALO_CHUNKS == 0
    tiles = l // tl
    const = lambda bi, i: (0, 0)
    row_spec = pl.BlockSpec((CHUNK, n_chunks, D_CONV), lambda bi, i: (0, bi * tiles + i, 0))
    halo_spec = pl.BlockSpec(
        (CHUNK, HALO_CHUNKS, D_CONV),
        lambda bi, i: (0, jnp.maximum((bi * tiles + i) * (n_chunks // HALO_CHUNKS) - 1, 0), 0))
    return pl.pallas_call(
        functools.partial(_conv_kernel, tl=tl, sub=sub),
        grid=(b, l // tl),
        in_specs=[row_spec, halo_spec,
                  pl.BlockSpec(conv_w.shape, const), pl.BlockSpec(conv_b.shape, const),
                  pl.BlockSpec(ln_g.shape, const), pl.BlockSpec(ln_b.shape, const)],
        out_specs=row_spec,
        out_shape=jax.ShapeDtypeStruct((CHUNK, b * l // CHUNK, D_CONV), BF16),
        scratch_shapes=[
            pltpu.VMEM((D_CONV // LANES, tl + HALO, LANES), F32),
            pltpu.VMEM((SUBLANES, D_CONV // LANES, tl + HALO, LANES), F32),
            pltpu.VMEM((D_CONV // LANES, tl, LANES), F32),
        ],
        compiler_params=pltpu.CompilerParams(
            dimension_semantics=("parallel", "arbitrary"), vmem_limit_bytes=VMEM_LIMIT),
        name="conv",
    )(vt, vt, conv_w, conv_b, ln_g, ln_b)


def _mix_kernel(x_ref, yst_ref, c_ref,
                wg_ref, bg_ref, wglu_ref, bglu_ref, wco_ref, bco_ref,
                wo_ref, bo_ref, l1g_ref, l1b_ref, wrh_ref, wrl_ref, br_ref,
                h_ref, route_ref, *, alpha):
    c_out = jnp.dot(c_ref[...], wco_ref[...], preferred_element_type=F32) + bco_ref[...]

    glu = lax.dot_general(yst_ref[...], wglu_ref[...], _TN, preferred_element_type=F32) + bglu_ref[...]
    s_out = glu[:, :D_MODEL] * jax.nn.sigmoid(glu[:, D_MODEL:])

    x = x_ref[...]
    gates = jnp.dot(x.astype(BF16), wg_ref[...], preferred_element_type=F32) + bg_ref[...]
    merged = jax.nn.sigmoid(gates[:, :D_MODEL]) * s_out + jax.nn.sigmoid(gates[:, D_MODEL:]) * c_out
    mix = jnp.dot(merged.astype(BF16), wo_ref[...], preferred_element_type=F32) + bo_ref[...]
    h = _layer_norm(alpha * x + mix, l1g_ref[...], l1b_ref[...])
    h_ref[...] = h

    h_hi = h.astype(BF16)
    h_lo = (h - h_hi.astype(F32)).astype(BF16)
    z = (jnp.dot(h_hi, wrh_ref[...], preferred_element_type=F32)
         + jnp.dot(h_lo, wrh_ref[...], preferred_element_type=F32)
         + jnp.dot(h_hi, wrl_ref[...], preferred_element_type=F32)) + br_ref[...]
    lane = lax.broadcasted_iota(jnp.int32, z.shape, 1)
    lanef = lane.astype(F32)
    neg = jnp.float32(-jnp.inf)
    big = jnp.float32(1e9)
    gmask = lane < N_GROUPS
    zg = jnp.where(gmask, z, neg)
    gmax = jnp.max(zg, axis=1, keepdims=True)
    gsel = jnp.min(jnp.where(gmask & (zg == gmax), lanef, big), axis=1, keepdims=True)
    p_group = 1.0 / jnp.sum(jnp.where(gmask, jnp.exp(zg - gmax), 0.0), axis=1, keepdims=True)
    lo = N_GROUPS + N_EXP_PER_GROUP * gsel
    emask = (lanef >= lo) & (lanef < lo + N_EXP_PER_GROUP)
    ze = jnp.where(emask, z, neg)
    v1 = jnp.max(ze, axis=1, keepdims=True)
    i1 = jnp.min(jnp.where(emask & (ze == v1), lanef, big), axis=1, keepdims=True)
    emask2 = emask & (lanef != i1)
    ze2 = jnp.where(emask2, z, neg)
    v2 = jnp.max(ze2, axis=1, keepdims=True)
    i2 = jnp.min(jnp.where(emask2 & (ze2 == v2), lanef, big), axis=1, keepdims=True)
    e2 = jnp.exp(v2 - v1)
    inv = 1.0 / (1.0 + e2)
    g1 = p_group * inv
    g2 = p_group * (e2 * inv)
    a = jnp.minimum(i1, i2) - lo
    b = jnp.maximum(i1, i2) - lo
    cls = gsel * N_PAIRS + (a * (2 * N_EXP_PER_GROUP - 1 - a)) * 0.5 + (b - a - 1.0)
    out = jnp.where(lane == 0, i1 - N_GROUPS, 0.0)
    out = jnp.where(lane == 1, i2 - N_GROUPS, out)
    out = jnp.where(lane == 2, g1, out)
    out = jnp.where(lane == 3, g2, out)
    out = jnp.where(lane == 4, cls, out)
    route_ref[...] = out


def _mix(x2, yst, ct, wts, rg, alpha):
    n_rows = x2.shape[0]

    def tau_spec(f):
        return pl.BlockSpec((None, rg, f), lambda i, t: (t, i, 0))

    def full_spec(a):
        nd = a.ndim
        return pl.BlockSpec(a.shape, lambda i, t: (0,) * nd)

    return pl.pallas_call(
        functools.partial(_mix_kernel, alpha=alpha),
        grid=(n_rows // rg, CHUNK),
        in_specs=[pl.BlockSpec((rg, D_MODEL), lambda i, t: (i, t)),
                  pl.BlockSpec((None, D_SSM, rg), lambda i, t: (t, 0, i)),
                  tau_spec(D_CONV)]
                 + [full_spec(a) for a in wts],
        out_specs=[tau_spec(D_MODEL), tau_spec(ROUTE_LANES)],
        out_shape=[
            jax.ShapeDtypeStruct((CHUNK, n_rows, D_MODEL), F32),
            jax.ShapeDtypeStruct((CHUNK, n_rows, ROUTE_LANES), F32),
        ],
        compiler_params=pltpu.CompilerParams(
            dimension_semantics=("parallel", "arbitrary"), vmem_limit_bytes=VMEM_LIMIT),
        name="mix",
    )(x2, yst, ct, *wts)


def _gather_rows(idx_ref, k, src_hbm, dst, sem, n):
    def issue(r, carry):
        pltpu.make_async_copy(src_hbm.at[pl.ds(idx_ref[0, k, r], 1)], dst.at[pl.ds(r, 1)], sem).start()
        return carry
    lax.fori_loop(0, n, issue, 0, unroll=8)


def _wait_rows(src_hbm, dst, sem, n):
    pltpu.make_async_copy(src_hbm.at[pl.ds(0, n)], dst, sem).wait()


def _expert(xb, wgu, wd):
    gu = jnp.dot(xb, wgu, preferred_element_type=F32)
    hid = jax.nn.silu(gu[:, :D_FF_EXPERT]) * gu[:, D_FF_EXPERT:]
    return jnp.dot(hid.astype(BF16), wd, preferred_element_type=F32)


def _moe_kernel(ea_ref, eb_ref, used_ref, tok_ref, tok_next_ref, h_hbm, wgu_a_ref, wgu_b_ref, wd_a_ref, wd_b_ref,
                y_ref, xbuf, sem, *, tm):
    del ea_ref, eb_ref
    i = pl.program_id(0)
    n = pl.num_programs(0)
    slot = i % 2
    nxt = jnp.minimum(i + 1, n - 1)

    @pl.when((i == 0) & (used_ref[0] == 1))
    def _():
        _gather_rows(tok_ref, 0, h_hbm, xbuf.at[0], sem.at[0], tm)

    @pl.when((i + 1 < n) & (used_ref[nxt] == 1))
    def _():
        _gather_rows(tok_next_ref, 0, h_hbm, xbuf.at[1 - slot], sem.at[1 - slot], tm)

    @pl.when(used_ref[i] == 1)
    def _():
        _wait_rows(h_hbm, xbuf.at[slot], sem.at[slot], tm)
        xb = xbuf[slot].astype(BF16)
        y_ref[:, :D_MODEL] = _expert(xb, wgu_a_ref[0], wd_a_ref[0])
        y_ref[:, D_MODEL:] = _expert(xb, wgu_b_ref[0], wd_b_ref[0])

    @pl.when(used_ref[i] == 0)
    def _():
        y_ref[...] = jnp.zeros_like(y_ref)


def _moe(blk_ea, blk_eb, blk_used, tok_blocks, h2, w_gu, w_d, tm):
    n_blocks = tok_blocks.shape[0]
    grid_spec = pltpu.PrefetchScalarGridSpec(
        num_scalar_prefetch=3,
        grid=(n_blocks,),
        in_specs=[
            pl.BlockSpec((1, 1, tm), lambda i, ea, eb, used: (i, 0, 0), memory_space=pltpu.SMEM),
            pl.BlockSpec((1, 1, tm), lambda i, ea, eb, used: (jnp.minimum(i + 1, n_blocks - 1), 0, 0),
                         memory_space=pltpu.SMEM),
            pl.BlockSpec(memory_space=pl.ANY),
            pl.BlockSpec((1, D_MODEL, 2 * D_FF_EXPERT), lambda i, ea, eb, used: (ea[i], 0, 0)),
            pl.BlockSpec((1, D_MODEL, 2 * D_FF_EXPERT), lambda i, ea, eb, used: (eb[i], 0, 0)),
            pl.BlockSpec((1, D_FF_EXPERT, D_MODEL), lambda i, ea, eb, used: (ea[i], 0, 0)),
            pl.BlockSpec((1, D_FF_EXPERT, D_MODEL), lambda i, ea, eb, used: (eb[i], 0, 0)),
        ],
        out_specs=pl.BlockSpec((tm, TOP_K_INNER * D_MODEL), lambda i, ea, eb, used: (i, 0)),
        scratch_shapes=[pltpu.VMEM((2, tm, D_MODEL), F32), pltpu.SemaphoreType.DMA((2,))],
    )
    return pl.pallas_call(
        functools.partial(_moe_kernel, tm=tm),
        grid_spec=grid_spec,
        out_shape=jax.ShapeDtypeStruct((n_blocks * tm, TOP_K_INNER * D_MODEL), F32),
        compiler_params=pltpu.CompilerParams(
            dimension_semantics=("arbitrary",), vmem_limit_bytes=VMEM_LIMIT),
        name="moe",
    )(blk_ea, blk_eb, blk_used, tok_blocks, tok_blocks, h2, w_gu, w_gu, w_d, w_d)


def _combine_kernel(pos_ref, pos_next_ref, h_ref, route_ref, y_hbm, g_ref, b_ref, o_ref, buf, res_ref, sem,
                    *, tt, alpha):
    i = pl.program_id(0)
    n = pl.num_programs(0)
    slot = i % 2

    @pl.when(i == 0)
    def _():
        _gather_rows(pos_ref, 0, y_hbm, buf.at[0], sem.at[0], tt)

    @pl.when(i + 1 < n)
    def _():
        _gather_rows(pos_next_ref, 0, y_hbm, buf.at[1 - slot], sem.at[1 - slot], tt)

    _wait_rows(y_hbm, buf.at[slot], sem.at[slot], tt)
    route = route_ref[...].reshape(tt, ROUTE_LANES)
    first_is_a = route[:, 0:1] < route[:, 1:2]
    g1 = route[:, 2:3]
    g2 = route[:, 3:4]
    wa = jnp.where(first_is_a, g1, g2)
    wb = jnp.where(first_is_a, g2, g1)
    moe = wa * buf[slot, :, :D_MODEL] + wb * buf[slot, :, D_MODEL:]
    res = _layer_norm(alpha * h_ref[...].reshape(tt, D_MODEL) + moe, g_ref[...], b_ref[...])
    n_r = tt // CHUNK
    n_cb = D_MODEL // LANES
    for c in range(n_cb):
        res_ref[c] = res[:, c * LANES:(c + 1) * LANES]
    for r in range(n_r):
        for c in range(n_cb):
            o_ref[r * CHUNK:(r + 1) * CHUNK, c * LANES:(c + 1) * LANES] = res_ref[c, pl.ds(r, CHUNK, stride=n_r), :]


def _combine(pos_blocks, ht, routet, y_buf, ln_g, ln_b, tt, alpha):
    n_tok = ht.shape[0] * ht.shape[1]
    n_tiles = n_tok // tt
    n_r = tt // CHUNK
    return pl.pallas_call(
        functools.partial(_combine_kernel, tt=tt, alpha=alpha),
        grid=(n_tiles,),
        in_specs=[
            pl.BlockSpec((1, 1, tt), lambda i: (i, 0, 0), memory_space=pltpu.SMEM),
            pl.BlockSpec((1, 1, tt), lambda i: (jnp.minimum(i + 1, n_tiles - 1), 0, 0),
                         memory_space=pltpu.SMEM),
            pl.BlockSpec((CHUNK, n_r, D_MODEL), lambda i: (0, i, 0)),
            pl.BlockSpec((CHUNK, n_r, ROUTE_LANES), lambda i: (0, i, 0)),
            pl.BlockSpec(memory_space=pl.ANY),
            pl.BlockSpec((1, D_MODEL), lambda i: (0, 0)),
            pl.BlockSpec((1, D_MODEL), lambda i: (0, 0)),
        ],
        out_specs=pl.BlockSpec((tt, D_MODEL), lambda i: (i, 0)),
        out_shape=jax.ShapeDtypeStruct((n_tok, D_MODEL), F32),
        scratch_shapes=[
            pltpu.VMEM((2, tt, TOP_K_INNER * D_MODEL), F32),
            pltpu.VMEM((D_MODEL // LANES, tt, LANES), F32),
            pltpu.SemaphoreType.DMA((2,)),
        ],
        compiler_params=pltpu.CompilerParams(
            dimension_semantics=("arbitrary",), vmem_limit_bytes=VMEM_LIMIT),
        name="combine",
    )(pos_blocks, pos_blocks, ht, routet, y_buf, ln_g, ln_b)


def _class_experts():
    ea, eb = [], []
    for g in range(N_GROUPS):
        for a in range(N_EXP_PER_GROUP):
            for b in range(a + 1, N_EXP_PER_GROUP):
                ea.append(g * N_EXP_PER_GROUP + a)
                eb.append(g * N_EXP_PER_GROUP + b)
    return jnp.asarray(ea, jnp.int32), jnp.asarray(eb, jnp.int32)


def _dispatch(cls, tm):
    n_tok = cls.shape[0]
    i32 = jnp.int32
    order = jnp.argsort(cls).astype(i32)
    inv = jnp.argsort(order).astype(i32)
    counts = jnp.sum(cls[:, None] == jnp.arange(N_CLASSES, dtype=i32)[None, :], axis=0).astype(i32)
    start = jnp.cumsum(counts) - counts
    pcounts = (counts + tm - 1) // tm * tm
    pend = jnp.cumsum(pcounts)
    pstart = pend - pcounts
    n_blocks = -(-n_tok // tm) + N_CLASSES
    blk_start = jnp.arange(n_blocks, dtype=i32) * tm
    blk_cls = jnp.minimum(jnp.sum(blk_start[:, None] >= pend[None, :], axis=1), N_CLASSES - 1).astype(i32)
    blk_used = (blk_start < pend[-1]).astype(i32)
    cls_ea, cls_eb = _class_experts()
    row = jnp.arange(n_blocks * tm, dtype=i32)
    row_c = jnp.repeat(blk_cls, tm)
    within = row - pstart[row_c]
    valid = within < counts[row_c]
    src = jnp.clip(start[row_c] + within, 0, n_tok - 1)
    tok_buf = jnp.where(valid, order[src], 0).astype(i32)
    pos = (pstart[cls] + inv - start[cls]).astype(i32)
    return cls_ea[blk_cls], cls_eb[blk_cls], blk_used, tok_buf.reshape(n_blocks, 1, tm), pos


def _layer(x, p, alpha):
    (w_in, b_in, lam_re, lam_im, log_dt, ssm_b_re, ssm_b_im, ssm_c_re, ssm_c_im,
     ssm_d, w_glu, b_glu, conv_w, conv_b, ln_c_g, ln_c_b, w_cout, b_cout, w_out, b_out,
     ln1_g, ln1_b, w_rg, b_rg, w_re, b_re, w_gate, w_up, w_down, ln2_g, ln2_b) = p
    bsz, seq, _ = x.shape
    n_tok = bsz * seq
    assert seq % CHUNK == 0
    nj = seq // CHUNK
    n_rows = bsz * nj
    assert nj & (nj - 1) == 0, "chunks per sequence must be a power of two"
    n_steps = max(1, int(math.log2(nj)))
    assert n_steps <= 8

    n_ucv = D_SSM + 2 * D_CONV
    row = lambda a: a.reshape(1, -1).astype(F32)
    rg = min(512, n_rows)

    x2 = x.reshape(n_rows, CHUNK * D_MODEL)
    ut, vt = _inproj(x2, w_in[:, :D_SSM].T.astype(BF16), b_in[:D_SSM].reshape(-1, 1).astype(F32),
                     w_in[:, D_SSM:n_ucv].astype(BF16), row(b_in[D_SSM:n_ucv]), rg)

    mzt, wt_state, coef = _s5_chunk_weights(lam_re, lam_im, log_dt, ssm_b_re, ssm_b_im,
                                            ssm_c_re, ssm_c_im, n_steps)
    dcol = jnp.tile(ssm_d.astype(F32).reshape(N_SSM_GROUPS, 1, SSM_GROUP), (1, CHUNK, 1))
    dcol = dcol.reshape(N_SSM_GROUPS, CHUNK_F, 1)
    yst = _s5(ut.reshape(CHUNK, N_SSM_GROUPS, SSM_GROUP, n_rows), mzt, wt_state, coef, dcol, nj, n_steps)
    yst = yst.reshape(CHUNK, D_SSM, n_rows)

    tl = min(512, seq)
    c_act = _conv(vt, bsz, seq, conv_w.reshape(CONV_WIDTH, D_CONV).astype(F32),
                  row(conv_b), row(ln_c_g), row(ln_c_b), tl)

    w_r = jnp.zeros((D_MODEL, ROUTE_LANES), F32)
    w_r = w_r.at[:, :N_GROUPS].set(w_rg).at[:, N_GROUPS:N_GROUPS + N_EXPERTS].set(w_re)
    b_r = jnp.zeros((1, ROUTE_LANES), F32)
    b_r = b_r.at[0, :N_GROUPS].set(b_rg).at[0, N_GROUPS:N_GROUPS + N_EXPERTS].set(b_re)
    w_r_hi = w_r.astype(BF16)
    w_r_lo = (w_r - w_r_hi.astype(F32)).astype(BF16)
    wts = (w_in[:, n_ucv:].astype(BF16), row(b_in[n_ucv:]),
           w_glu.astype(BF16), row(b_glu),
           w_cout.astype(BF16), row(b_cout), w_out.astype(BF16), row(b_out),
           row(ln1_g), row(ln1_b), w_r_hi, w_r_lo, b_r)
    ht, routet = _mix(x2, yst, c_act, wts, rg, alpha)

    tm_moe = 128
    cls = routet[:, :, 4].reshape(n_tok).astype(jnp.int32)
    blk_ea, blk_eb, blk_used, tok_blocks, pos = _dispatch(cls, tm_moe)
    w_gu = jnp.concatenate([w_gate, w_up], axis=-1).astype(BF16)
    y_buf = _moe(blk_ea, blk_eb, blk_used, tok_blocks, ht.reshape(n_tok, D_MODEL), w_gu,
                 w_down.astype(BF16), tm_moe)
    tt = CHUNK * CHUNK
    n_r = tt // CHUNK
    pos_blocks = pos.reshape(CHUNK, n_rows // n_r, n_r).transpose(1, 0, 2).reshape(n_rows // n_r, 1, tt)
    out = _combine(pos_blocks, ht, routet, y_buf, row(ln2_g), row(ln2_b), tt, alpha)
    return out.reshape(bsz, seq, D_MODEL)


def kernel(x, w_in, b_in, lam_re, lam_im, log_dt, ssm_b_re, ssm_b_im, ssm_c_re, ssm_c_im, ssm_d, w_glu, b_glu, conv_w, conv_b, ln_c_g, ln_c_b, w_cout, b_cout, w_out, b_out, ln1_g, ln1_b, w_route_group, b_route_group, w_route_expert, b_route_expert, w_gate, w_up, w_down, ln2_g, ln2_b):
    params = (w_in, b_in, lam_re, lam_im, log_dt, ssm_b_re, ssm_b_im, ssm_c_re, ssm_c_im,
              ssm_d, w_glu, b_glu, conv_w, conv_b, ln_c_g, ln_c_b, w_cout, b_cout, w_out, b_out,
              ln1_g, ln1_b, w_route_group, b_route_group, w_route_expert, b_route_expert,
              w_gate, w_up, w_down, ln2_g, ln2_b)
    depth = w_in.shape[0]
    alpha = (2.0 * depth) ** 0.25
    h = x
    for i in range(depth):
        h = _layer(h, tuple(p[i] for p in params), alpha)
    return h
```

```python
import functools
import math

import jax
import jax.numpy as jnp
from jax import lax
from jax.experimental import pallas as pl
from jax.experimental.pallas import tpu as pltpu

F32 = jnp.float32
BF16 = jnp.bfloat16
HIGHEST = lax.Precision.HIGHEST

D_MODEL = 1024
D_SSM = 512
SSM_GROUP = 16
N_SSM_GROUPS = D_SSM // SSM_GROUP
SSM_STATE = 64
D_CONV = 512
CONV_WIDTH = 31
N_GROUPS = 4
N_EXP_PER_GROUP = 8
N_EXPERTS = N_GROUPS * N_EXP_PER_GROUP
TOP_K_INNER = 2
N_PAIRS = N_EXP_PER_GROUP * (N_EXP_PER_GROUP - 1) // 2
N_CLASSES = N_GROUPS * N_PAIRS
D_FF_EXPERT = D_MODEL // 4
LN_EPS = 1e-5

CHUNK = 16
CHUNK_F = CHUNK * SSM_GROUP
STATE_F = 2 * SSM_STATE
ROUTE_LANES = 128
HALO = 32
HALO_CHUNKS = 8
SUBLANES = 8
LANES = 128
TOKEN_SUBLANES = D_MODEL // LANES
PAIR_SUBLANES = TOP_K_INNER * TOKEN_SUBLANES
VMEM_LIMIT = 56 * 1024 * 1024

_NT = (((1,), (1,)), ((), ()))
_TN = (((0,), (0,)), ((), ()))


def _layer_norm(x, g, b):
    mu = jnp.mean(x, axis=-1, keepdims=True)
    xc = x - mu
    var = jnp.mean(xc * xc, axis=-1, keepdims=True)
    return xc * lax.rsqrt(var + LN_EPS) * g + b


def _inproj_kernel(x_ref, wut_ref, but_ref, wcv_ref, bcv_ref, ut_ref, v_ref):
    x = x_ref[...].astype(BF16)
    ut = lax.dot_general(wut_ref[...], x, _NT, preferred_element_type=F32) + but_ref[...]
    ut_ref[...] = ut.astype(BF16)
    p = jnp.dot(x, wcv_ref[...], preferred_element_type=F32) + bcv_ref[...]
    v_ref[...] = p[:, :D_CONV] * jax.nn.sigmoid(p[:, D_CONV:])


def _inproj(x2, w_ut, b_ut, w_cv, b_cv, rg):
    n_rows = x2.shape[0]
    const = lambda i, t: (0, 0)
    return pl.pallas_call(
        _inproj_kernel,
        grid=(n_rows // rg, CHUNK),
        in_specs=[
            pl.BlockSpec((rg, D_MODEL), lambda i, t: (i, t)),
            pl.BlockSpec(w_ut.shape, const),
            pl.BlockSpec(b_ut.shape, const),
            pl.BlockSpec(w_cv.shape, const),
            pl.BlockSpec(b_cv.shape, const),
        ],
        out_specs=[
            pl.BlockSpec((None, D_SSM, rg), lambda i, t: (t, 0, i)),
            pl.BlockSpec((None, rg, D_CONV), lambda i, t: (t, i, 0)),
        ],
        out_shape=[
            jax.ShapeDtypeStruct((CHUNK, D_SSM, n_rows), BF16),
            jax.ShapeDtypeStruct((CHUNK, n_rows, D_CONV), F32),
        ],
        compiler_params=pltpu.CompilerParams(
            dimension_semantics=("parallel", "arbitrary"), vmem_limit_bytes=VMEM_LIMIT),
        name="inproj",
    )(x2, w_ut, b_ut, w_cv, b_cv)


def _s5_chunk_weights(lam_re, lam_im, log_dt, b_re, b_im, c_re, c_im, n_steps):
    f32 = F32
    lr, li = lam_re.astype(f32), lam_im.astype(f32)
    dt = jnp.exp(log_dt.astype(f32))[:, None]
    mag = jnp.exp(lr * dt)
    ar, ai = mag * jnp.cos(li * dt), mag * jnp.sin(li * dt)
    den = lr * lr + li * li
    fr = ((ar - 1.0) * lr + ai * li) / den
    fi = (ai * lr - (ar - 1.0) * li) / den
    br, bi = b_re.astype(f32), b_im.astype(f32)
    bbr = fr[..., None] * br - fi[..., None] * bi
    bbi = fr[..., None] * bi + fi[..., None] * br
    cr, ci = c_re.astype(f32), c_im.astype(f32)

    pr, pi = [jnp.ones_like(ar)], [jnp.zeros_like(ai)]
    for _ in range(CHUNK):
        pr_n = pr[-1] * ar - pi[-1] * ai
        pi_n = pr[-1] * ai + pi[-1] * ar
        pr.append(pr_n)
        pi.append(pi_n)
    pw_r, pw_i = jnp.stack(pr), jnp.stack(pi)

    ca_r = cr[None] * pw_r[:, :, None, :] - ci[None] * pw_i[:, :, None, :]
    ca_i = cr[None] * pw_i[:, :, None, :] + ci[None] * pw_r[:, :, None, :]
    kk = (jnp.einsum('kghp,gpq->kghq', ca_r[:CHUNK], bbr, precision=HIGHEST)
          - jnp.einsum('kghp,gpq->kghq', ca_i[:CHUNK], bbi, precision=HIGHEST))
    t_in = jnp.arange(CHUNK)[:, None]
    t_out = jnp.arange(CHUNK)[None, :]
    lag = t_out - t_in
    m = kk[jnp.maximum(lag, 0)]
    m = jnp.where((lag >= 0)[:, :, None, None, None], m, 0.0)
    m = m.transpose(2, 1, 3, 0, 4).reshape(N_SSM_GROUPS, CHUNK_F, CHUNK_F)

    rp_r, rp_i = pw_r[CHUNK - 1::-1][:CHUNK], pw_i[CHUNK - 1::-1][:CHUNK]
    z_r = rp_r[..., None] * bbr[None] - rp_i[..., None] * bbi[None]
    z_i = rp_r[..., None] * bbi[None] + rp_i[..., None] * bbr[None]
    z = jnp.concatenate([z_r, z_i], axis=2)
    z = z.transpose(1, 2, 0, 3).reshape(N_SSM_GROUPS, STATE_F, CHUNK_F)
    mzt = jnp.concatenate([m, z], axis=1)

    w = jnp.concatenate([ca_r[1:], -ca_i[1:]], axis=-1)
    wt = w.transpose(1, 0, 2, 3).reshape(N_SSM_GROUPS, CHUNK_F, STATE_F)

    sr, si = pw_r[CHUNK], pw_i[CHUNK]
    c1, c2 = [], []
    for _ in range(n_steps):
        c1.append(jnp.concatenate([sr, sr], axis=-1)[None])
        c2.append(jnp.concatenate([-si, si], axis=-1)[None])
        sr, si = sr * sr - si * si, 2.0 * sr * si
    pad = jnp.zeros((8 - n_steps, N_SSM_GROUPS, STATE_F), f32)
    coef = jnp.concatenate(c1 + [pad] + c2 + [pad], axis=0).transpose(1, 2, 0)
    return mzt.astype(BF16), wt.astype(BF16), coef


def _s5_kernel(ut_ref, mzt_ref, wt_ref, coef_ref, dcol_ref, o_ref, *, nj, n_steps):
    ut = jnp.concatenate([ut_ref[t] for t in range(CHUNK)], axis=0)
    r = jnp.dot(mzt_ref[0], ut, preferred_element_type=F32)
    y0 = r[:CHUNK_F]
    t = r[CHUNK_F:]
    j = lax.broadcasted_iota(jnp.int32, t.shape, 1) & (nj - 1)
    for m in range(n_steps):
        d = 1 << m
        sh = jnp.where(j >= d, pltpu.roll(t, d, axis=1), 0.0)
        p1 = coef_ref[0, :, m:m + 1]
        p2 = coef_ref[0, :, 8 + m:9 + m]
        t = t + p1 * sh + p2 * pltpu.roll(sh, SSM_STATE, axis=0)
    s_prev = jnp.where(j >= 1, pltpu.roll(t, 1, axis=1), 0.0)
    y = y0 + jnp.dot(wt_ref[0], s_prev.astype(BF16), preferred_element_type=F32)
    ys = jax.nn.gelu(y + dcol_ref[0] * ut.astype(F32)).astype(BF16)
    for t_out in range(CHUNK):
        o_ref[t_out] = ys[t_out * SSM_GROUP:(t_out + 1) * SSM_GROUP]


def _s5(ut4, mzt, wt, coef, dcol, nj, n_steps):
    _, g, _, n_rows = ut4.shape
    blk = pl.BlockSpec((CHUNK, None, SSM_GROUP, n_rows), lambda i: (0, i, 0, 0))
    return pl.pallas_call(
        functools.partial(_s5_kernel, nj=nj, n_steps=n_steps),
        grid=(g,),
        in_specs=[
            blk,
            pl.BlockSpec((1, CHUNK_F + STATE_F, CHUNK_F), lambda i: (i, 0, 0)),
            pl.BlockSpec((1, CHUNK_F, STATE_F), lambda i: (i, 0, 0)),
            pl.BlockSpec((1, STATE_F, 16), lambda i: (i, 0, 0)),
            pl.BlockSpec((1, CHUNK_F, 1), lambda i: (i, 0, 0)),
        ],
        out_specs=blk,
        out_shape=jax.ShapeDtypeStruct(ut4.shape, BF16),
        compiler_params=pltpu.CompilerParams(
            dimension_semantics=("parallel",), vmem_limit_bytes=VMEM_LIMIT),
        name="s5",
    )(ut4, mzt, wt, coef, dcol)


def _conv_kernel(v_ref, vh_ref, cw_ref, cb_ref, g_ref, b_ref, c_ref, vext_ref, shift_ref, conv_ref, *, tl, sub):
    i = pl.program_id(1)
    n_ext = tl + HALO
    n_chunks = tl // CHUNK
    n_hist = HALO // CHUNK
    blocks = [slice(c * LANES, (c + 1) * LANES) for c in range(D_CONV // LANES)]
    for t in range(CHUNK):
        for c, lanes in enumerate(blocks):
            hist = vh_ref[t, HALO_CHUNKS - n_hist:HALO_CHUNKS, lanes]
            vext_ref[c, pl.ds(t, n_hist, stride=CHUNK), :] = jnp.where(i > 0, hist, 0.0)
            vext_ref[c, pl.ds(HALO + t, n_chunks, stride=CHUNK), :] = v_ref[t, :, lanes]
    for m in range(SUBLANES):
        shift_ref[m, :, 0:n_ext - m, :] = vext_ref[:, m:n_ext, :]
    off = HALO - (CONV_WIDTH - 1)
    for c, lanes in enumerate(blocks):
        for s in range(tl // sub):
            acc = jnp.zeros((sub, LANES), F32) + cb_ref[:, lanes]
            for k in range(CONV_WIDTH):
                q, m = divmod(off + k, SUBLANES)
                r0 = s * sub + q * SUBLANES
                acc = acc + cw_ref[k:k + 1, lanes] * shift_ref[m, c, r0:r0 + sub, :]
            conv_ref[c, s * sub:(s + 1) * sub, :] = acc
    xs = [conv_ref[c] for c in range(len(blocks))]
    mu = sum(jnp.sum(x, axis=-1, keepdims=True) for x in xs) * (1.0 / D_CONV)
    xc = [x - mu for x in xs]
    var = sum(jnp.sum(x * x, axis=-1, keepdims=True) for x in xc) * (1.0 / D_CONV)
    inv = lax.rsqrt(var + LN_EPS)
    for c, lanes in enumerate(blocks):
        conv_ref[c] = jax.nn.silu(xc[c] * inv * g_ref[:, lanes] + b_ref[:, lanes])
    for t in range(CHUNK):
        for c, lanes in enumerate(blocks):
            c_ref[t, :, lanes] = conv_ref[c, pl.ds(t, n_chunks, stride=CHUNK), :].astype(BF16)


def _conv(vt, b, l, conv_w, conv_b, ln_g, ln_b, tl):
    sub = min(64, tl)
    n_chunks = tl // CHUNK
    assert n_chunks % HALO_CHUNKS == 0
    tiles = l // tl
    const = lambda bi, i: (0, 0)
    row_spec = pl.BlockSpec((CHUNK, n_chunks, D_CONV), lambda bi, i: (0, bi * tiles + i, 0))
    halo_spec = pl.BlockSpec(
        (CHUNK, HALO_CHUNKS, D_CONV),
        lambda bi, i: (0, jnp.maximum((bi * tiles + i) * (n_chunks // HALO_CHUNKS) - 1, 0), 0))
    return pl.pallas_call(
        functools.partial(_conv_kernel, tl=tl, sub=sub),
        grid=(b, l // tl),
        in_specs=[row_spec, halo_spec,
                  pl.BlockSpec(conv_w.shape, const), pl.BlockSpec(conv_b.shape, const),
                  pl.BlockSpec(ln_g.shape, const), pl.BlockSpec(ln_b.shape, const)],
        out_specs=row_spec,
        out_shape=jax.ShapeDtypeStruct((CHUNK, b * l // CHUNK, D_CONV), BF16),
        scratch_shapes=[
            pltpu.VMEM((D_CONV // LANES, tl + HALO, LANES), F32),
            pltpu.VMEM((SUBLANES, D_CONV // LANES, tl + HALO, LANES), F32),
            pltpu.VMEM((D_CONV // LANES, tl, LANES), F32),
        ],
        compiler_params=pltpu.CompilerParams(
            dimension_semantics=("parallel", "arbitrary"), vmem_limit_bytes=VMEM_LIMIT),
        name="conv",
    )(vt, vt, conv_w, conv_b, ln_g, ln_b)


def _mix_kernel(x_ref, yst_ref, c_ref,
                wg_ref, bg_ref, wglu_ref, bglu_ref, wco_ref, bco_ref,
                wo_ref, bo_ref, l1g_ref, l1b_ref, wrh_ref, wrl_ref, br_ref,
                h_ref, route_ref, *, alpha):
    c_out = jnp.dot(c_ref[...], wco_ref[...], preferred_element_type=F32) + bco_ref[...]

    glu = lax.dot_general(yst_ref[...], wglu_ref[...], _TN, preferred_element_type=F32) + bglu_ref[...]
    s_out = glu[:, :D_MODEL] * jax.nn.sigmoid(glu[:, D_MODEL:])

    x = x_ref[...]
    gates = jnp.dot(x.astype(BF16), wg_ref[...], preferred_element_type=F32) + bg_ref[...]
    merged = jax.nn.sigmoid(gates[:, :D_MODEL]) * s_out + jax.nn.sigmoid(gates[:, D_MODEL:]) * c_out
    mix = jnp.dot(merged.astype(BF16), wo_ref[...], preferred_element_type=F32) + bo_ref[...]
    h = _layer_norm(alpha * x + mix, l1g_ref[...], l1b_ref[...])
    rg = h.shape[0]
    for s in range(TOKEN_SUBLANES):
        h_ref[pl.ds(s, rg, stride=TOKEN_SUBLANES), :] = h[:, s * LANES:(s + 1) * LANES]

    h_hi = h.astype(BF16)
    h_lo = (h - h_hi.astype(F32)).astype(BF16)
    z = (jnp.dot(h_hi, wrh_ref[...], preferred_element_type=F32)
         + jnp.dot(h_lo, wrh_ref[...], preferred_element_type=F32)
         + jnp.dot(h_hi, wrl_ref[...], preferred_element_type=F32)) + br_ref[...]
    lane = lax.broadcasted_iota(jnp.int32, z.shape, 1)
    lanef = lane.astype(F32)
    neg = jnp.float32(-jnp.inf)
    big = jnp.float32(1e9)
    gmask = lane < N_GROUPS
    zg = jnp.where(gmask, z, neg)
    gmax = jnp.max(zg, axis=1, keepdims=True)
    gsel = jnp.min(jnp.where(gmask & (zg == gmax), lanef, big), axis=1, keepdims=True)
    p_group = 1.0 / jnp.sum(jnp.where(gmask, jnp.exp(zg - gmax), 0.0), axis=1, keepdims=True)
    lo = N_GROUPS + N_EXP_PER_GROUP * gsel
    emask = (lanef >= lo) & (lanef < lo + N_EXP_PER_GROUP)
    ze = jnp.where(emask, z, neg)
    v1 = jnp.max(ze, axis=1, keepdims=True)
    i1 = jnp.min(jnp.where(emask & (ze == v1), lanef, big), axis=1, keepdims=True)
    emask2 = emask & (lanef != i1)
    ze2 = jnp.where(emask2, z, neg)
    v2 = jnp.max(ze2, axis=1, keepdims=True)
    i2 = jnp.min(jnp.where(emask2 & (ze2 == v2), lanef, big), axis=1, keepdims=True)
    e2 = jnp.exp(v2 - v1)
    inv = 1.0 / (1.0 + e2)
    g1 = p_group * inv
    g2 = p_group * (e2 * inv)
    a = jnp.minimum(i1, i2) - lo
    b = jnp.maximum(i1, i2) - lo
    cls = gsel * N_PAIRS + (a * (2 * N_EXP_PER_GROUP - 1 - a)) * 0.5 + (b - a - 1.0)
    out = jnp.where(lane == 0, i1 - N_GROUPS, 0.0)
    out = jnp.where(lane == 1, i2 - N_GROUPS, out)
    out = jnp.where(lane == 2, g1, out)
    out = jnp.where(lane == 3, g2, out)
    out = jnp.where(lane == 4, cls, out)
    route_ref[...] = out


def _mix(x2, yst, ct, wts, rg, alpha):
    n_rows = x2.shape[0]

    def tau_spec(f):
        return pl.BlockSpec((None, rg, f), lambda i, t: (t, i, 0))

    def full_spec(a):
        nd = a.ndim
        return pl.BlockSpec(a.shape, lambda i, t: (0,) * nd)

    return pl.pallas_call(
        functools.partial(_mix_kernel, alpha=alpha),
        grid=(n_rows // rg, CHUNK),
        in_specs=[pl.BlockSpec((rg, D_MODEL), lambda i, t: (i, t)),
                  pl.BlockSpec((None, D_SSM, rg), lambda i, t: (t, 0, i)),
                  tau_spec(D_CONV)]
                 + [full_spec(a) for a in wts],
        out_specs=[pl.BlockSpec((None, rg * TOKEN_SUBLANES, LANES), lambda i, t: (t, i, 0)),
                   tau_spec(ROUTE_LANES)],
        out_shape=[
            jax.ShapeDtypeStruct((CHUNK, n_rows * TOKEN_SUBLANES, LANES), F32),
            jax.ShapeDtypeStruct((CHUNK, n_rows, ROUTE_LANES), F32),
        ],
        compiler_params=pltpu.CompilerParams(
            dimension_semantics=("parallel", "arbitrary"), vmem_limit_bytes=VMEM_LIMIT),
        name="mix",
    )(x2, yst, ct, *wts)


def _gather_rows(idx_ref, src_hbm, dst, sem, n, rows_per):
    def issue(r, carry):
        src0 = pl.multiple_of(idx_ref[0, 0, r] * rows_per, rows_per)
        dst0 = pl.multiple_of(r * rows_per, rows_per)
        pltpu.make_async_copy(src_hbm.at[pl.ds(src0, rows_per)], dst.at[pl.ds(dst0, rows_per)], sem).start()
        return carry
    lax.fori_loop(0, n, issue, 0, unroll=8)


def _wait_rows(src_hbm, dst, sem):
    pltpu.make_async_copy(src_hbm.at[pl.ds(0, dst.shape[0])], dst, sem).wait()


def _lane_blocks(ref2d, n, rows_per):
    return jnp.concatenate([ref2d[pl.ds(s, n, stride=rows_per), :] for s in range(rows_per)], axis=1)


def _expert(xb, wgu, wd):
    gu = jnp.dot(xb, wgu, preferred_element_type=F32)
    hid = jax.nn.silu(gu[:, :D_FF_EXPERT]) * gu[:, D_FF_EXPERT:]
    return jnp.dot(hid.astype(BF16), wd, preferred_element_type=F32)


def _moe_kernel(ea_ref, eb_ref, used_ref, tok_ref, tok_next_ref, h_hbm, wgu_a_ref, wgu_b_ref, wd_a_ref, wd_b_ref,
                y_ref, xbuf, sem, *, tm):
    del ea_ref, eb_ref
    i = pl.program_id(0)
    n = pl.num_programs(0)
    slot = i % 2
    nxt = jnp.minimum(i + 1, n - 1)

    @pl.when((i == 0) & (used_ref[0] == 1))
    def _():
        _gather_rows(tok_ref, h_hbm, xbuf.at[0], sem.at[0], tm, TOKEN_SUBLANES)

    @pl.when((i + 1 < n) & (used_ref[nxt] == 1))
    def _():
        _gather_rows(tok_next_ref, h_hbm, xbuf.at[1 - slot], sem.at[1 - slot], tm, TOKEN_SUBLANES)

    @pl.when(used_ref[i] == 1)
    def _():
        _wait_rows(h_hbm, xbuf.at[slot], sem.at[slot])
        xb = _lane_blocks(xbuf.at[slot], tm, TOKEN_SUBLANES).astype(BF16)
        ya = _expert(xb, wgu_a_ref[0], wd_a_ref[0])
        yb = _expert(xb, wgu_b_ref[0], wd_b_ref[0])
        for k, y in enumerate((ya, yb)):
            for s in range(TOKEN_SUBLANES):
                y_ref[pl.ds(k * TOKEN_SUBLANES + s, tm, stride=PAIR_SUBLANES), :] = y[:, s * LANES:(s + 1) * LANES]

    @pl.when(used_ref[i] == 0)
    def _():
        y_ref[...] = jnp.zeros_like(y_ref)


def _moe(blk_ea, blk_eb, blk_used, tok_blocks, h2, w_gu, w_d, tm):
    n_blocks = tok_blocks.shape[0]
    grid_spec = pltpu.PrefetchScalarGridSpec(
        num_scalar_prefetch=3,
        grid=(n_blocks,),
        in_specs=[
            pl.BlockSpec((1, 1, tm), lambda i, ea, eb, used: (i, 0, 0), memory_space=pltpu.SMEM),
            pl.BlockSpec((1, 1, tm), lambda i, ea, eb, used: (jnp.minimum(i + 1, n_blocks - 1), 0, 0),
                         memory_space=pltpu.SMEM),
            pl.BlockSpec(memory_space=pl.ANY),
            pl.BlockSpec((1, D_MODEL, 2 * D_FF_EXPERT), lambda i, ea, eb, used: (ea[i], 0, 0)),
            pl.BlockSpec((1, D_MODEL, 2 * D_FF_EXPERT), lambda i, ea, eb, used: (eb[i], 0, 0)),
            pl.BlockSpec((1, D_FF_EXPERT, D_MODEL), lambda i, ea, eb, used: (ea[i], 0, 0)),
            pl.BlockSpec((1, D_FF_EXPERT, D_MODEL), lambda i, ea, eb, used: (eb[i], 0, 0)),
        ],
        out_specs=pl.BlockSpec((tm * PAIR_SUBLANES, LANES), lambda i, ea, eb, used: (i, 0)),
        scratch_shapes=[pltpu.VMEM((2, tm * TOKEN_SUBLANES, LANES), F32), pltpu.SemaphoreType.DMA((2,))],
    )
    return pl.pallas_call(
        functools.partial(_moe_kernel, tm=tm),
        grid_spec=grid_spec,
        out_shape=jax.ShapeDtypeStruct((n_blocks * tm * PAIR_SUBLANES, LANES), F32),
        compiler_params=pltpu.CompilerParams(
            dimension_semantics=("arbitrary",), vmem_limit_bytes=VMEM_LIMIT),
        name="moe",
    )(blk_ea, blk_eb, blk_used, tok_blocks, tok_blocks, h2, w_gu, w_gu, w_d, w_d)


def _combine_kernel(pos_ref, pos_next_ref, h_ref, route_ref, y_hbm, g_ref, b_ref, o_ref, buf, res_ref, sem,
                    *, tt, alpha):
    i = pl.program_id(0)
    n = pl.num_programs(0)
    slot = i % 2
    n_r = tt // CHUNK

    @pl.when(i == 0)
    def _():
        _gather_rows(pos_ref, y_hbm, buf.at[0], sem.at[0], tt, PAIR_SUBLANES)

    @pl.when(i + 1 < n)
    def _():
        _gather_rows(pos_next_ref, y_hbm, buf.at[1 - slot], sem.at[1 - slot], tt, PAIR_SUBLANES)

    _wait_rows(y_hbm, buf.at[slot], sem.at[slot])
    y2 = _lane_blocks(buf.at[slot], tt, PAIR_SUBLANES)
    h = jnp.concatenate([_lane_blocks(h_ref.at[t], n_r, TOKEN_SUBLANES) for t in range(CHUNK)], axis=0)
    route = route_ref[...].reshape(tt, ROUTE_LANES)
    first_is_a = route[:, 0:1] < route[:, 1:2]
    g1 = route[:, 2:3]
    g2 = route[:, 3:4]
    wa = jnp.where(first_is_a, g1, g2)
    wb = jnp.where(first_is_a, g2, g1)
    moe = wa * y2[:, :D_MODEL] + wb * y2[:, D_MODEL:]
    res = _layer_norm(alpha * h + moe, g_ref[...], b_ref[...])
    n_cb = D_MODEL // LANES
    for c in range(n_cb):
        res_ref[c] = res[:, c * LANES:(c + 1) * LANES]
    for r in range(n_r):
        for c in range(n_cb):
            o_ref[r * CHUNK:(r + 1) * CHUNK, c * LANES:(c + 1) * LANES] = res_ref[c, pl.ds(r, CHUNK, stride=n_r), :]


def _combine(pos_blocks, ht, routet, y_buf, ln_g, ln_b, tt, alpha):
    n_tok = routet.shape[0] * routet.shape[1]
    n_tiles = n_tok // tt
    n_r = tt // CHUNK
    return pl.pallas_call(
        functools.partial(_combine_kernel, tt=tt, alpha=alpha),
        grid=(n_tiles,),
        in_specs=[
            pl.BlockSpec((1, 1, tt), lambda i: (i, 0, 0), memory_space=pltpu.SMEM),
            pl.BlockSpec((1, 1, tt), lambda i: (jnp.minimum(i + 1, n_tiles - 1), 0, 0),
                         memory_space=pltpu.SMEM),
            pl.BlockSpec((CHUNK, n_r * TOKEN_SUBLANES, LANES), lambda i: (0, i, 0)),
            pl.BlockSpec((CHUNK, n_r, ROUTE_LANES), lambda i: (0, i, 0)),
            pl.BlockSpec(memory_space=pl.ANY),
            pl.BlockSpec((1, D_MODEL), lambda i: (0, 0)),
            pl.BlockSpec((1, D_MODEL), lambda i: (0, 0)),
        ],
        out_specs=pl.BlockSpec((tt, D_MODEL), lambda i: (i, 0)),
        out_shape=jax.ShapeDtypeStruct((n_tok, D_MODEL), F32),
        scratch_shapes=[
            pltpu.VMEM((2, tt * PAIR_SUBLANES, LANES), F32),
            pltpu.VMEM((D_MODEL // LANES, tt, LANES), F32),
            pltpu.SemaphoreType.DMA((2,)),
        ],
        compiler_params=pltpu.CompilerParams(
            dimension_semantics=("arbitrary",), vmem_limit_bytes=VMEM_LIMIT),
        name="combine",
    )(pos_blocks, pos_blocks, ht, routet, y_buf, ln_g, ln_b)


def _class_experts():
    ea, eb = [], []
    for g in range(N_GROUPS):
        for a in range(N_EXP_PER_GROUP):
            for b in range(a + 1, N_EXP_PER_GROUP):
                ea.append(g * N_EXP_PER_GROUP + a)
                eb.append(g * N_EXP_PER_GROUP + b)
    return jnp.asarray(ea, jnp.int32), jnp.asarray(eb, jnp.int32)


def _lookup(table, idx):
    n = table.shape[0]
    hit = idx[..., None] == jnp.arange(n, dtype=idx.dtype)
    return jnp.sum(jnp.where(hit, table, 0), axis=-1).astype(table.dtype)


def _dispatch(cls, tm):
    n_tok = cls.shape[0]
    i32 = jnp.int32
    order = jnp.argsort(cls).astype(i32)
    inv = jnp.argsort(order).astype(i32)
    counts = jnp.sum(cls[:, None] == jnp.arange(N_CLASSES, dtype=i32)[None, :], axis=0).astype(i32)
    start = jnp.cumsum(counts) - counts
    pcounts = (counts + tm - 1) // tm * tm
    pend = jnp.cumsum(pcounts)
    pstart = pend - pcounts
    n_blocks = -(-n_tok // tm) + N_CLASSES
    blk_start = jnp.arange(n_blocks, dtype=i32) * tm
    blk_cls = jnp.minimum(jnp.sum(blk_start[:, None] >= pend[None, :], axis=1), N_CLASSES - 1).astype(i32)
    blk_used = (blk_start < pend[-1]).astype(i32)
    cls_ea, cls_eb = _class_experts()
    within = (blk_start - _lookup(pstart, blk_cls))[:, None] + jnp.arange(tm, dtype=i32)[None, :]
    valid = within < _lookup(counts, blk_cls)[:, None]
    src = jnp.clip(_lookup(start, blk_cls)[:, None] + within, 0, n_tok - 1)
    tok_buf = jnp.where(valid, order[src.reshape(-1)].reshape(n_blocks, tm), 0).astype(i32)
    pos = (_lookup(pstart - start, cls) + inv).astype(i32)
    return (_lookup(cls_ea, blk_cls), _lookup(cls_eb, blk_cls), blk_used,
            tok_buf.reshape(n_blocks, 1, tm), pos)


def _layer(x, p, alpha):
    (w_in, b_in, lam_re, lam_im, log_dt, ssm_b_re, ssm_b_im, ssm_c_re, ssm_c_im,
     ssm_d, w_glu, b_glu, conv_w, conv_b, ln_c_g, ln_c_b, w_cout, b_cout, w_out, b_out,
     ln1_g, ln1_b, w_rg, b_rg, w_re, b_re, w_gate, w_up, w_down, ln2_g, ln2_b) = p
    bsz, seq, _ = x.shape
    n_tok = bsz * seq
    assert seq % CHUNK == 0
    nj = seq // CHUNK
    n_rows = bsz * nj
    assert nj & (nj - 1) == 0, "chunks per sequence must be a power of two"
    n_steps = max(1, int(math.log2(nj)))
    assert n_steps <= 8

    n_ucv = D_SSM + 2 * D_CONV
    row = lambda a: a.reshape(1, -1).astype(F32)
    rg = min(512, n_rows)

    x2 = x.reshape(n_rows, CHUNK * D_MODEL)
    ut, vt = _inproj(x2, w_in[:, :D_SSM].T.astype(BF16), b_in[:D_SSM].reshape(-1, 1).astype(F32),
                     w_in[:, D_SSM:n_ucv].astype(BF16), row(b_in[D_SSM:n_ucv]), rg)

    mzt, wt_state, coef = _s5_chunk_weights(lam_re, lam_im, log_dt, ssm_b_re, ssm_b_im,
                                            ssm_c_re, ssm_c_im, n_steps)
    dcol = jnp.tile(ssm_d.astype(F32).reshape(N_SSM_GROUPS, 1, SSM_GROUP), (1, CHUNK, 1))
    dcol = dcol.reshape(N_SSM_GROUPS, CHUNK_F, 1)
    yst = _s5(ut.reshape(CHUNK, N_SSM_GROUPS, SSM_GROUP, n_rows), mzt, wt_state, coef, dcol, nj, n_steps)
    yst = yst.reshape(CHUNK, D_SSM, n_rows)

    tl = min(512, seq)
    c_act = _conv(vt, bsz, seq, conv_w.reshape(CONV_WIDTH, D_CONV).astype(F32),
                  row(conv_b), row(ln_c_g), row(ln_c_b), tl)

    w_r = jnp.zeros((D_MODEL, ROUTE_LANES), F32)
    w_r = w_r.at[:, :N_GROUPS].set(w_rg).at[:, N_GROUPS:N_GROUPS + N_EXPERTS].set(w_re)
    b_r = jnp.zeros((1, ROUTE_LANES), F32)
    b_r = b_r.at[0, :N_GROUPS].set(b_rg).at[0, N_GROUPS:N_GROUPS + N_EXPERTS].set(b_re)
    w_r_hi = w_r.astype(BF16)
    w_r_lo = (w_r - w_r_hi.astype(F32)).astype(BF16)
    wts = (w_in[:, n_ucv:].astype(BF16), row(b_in[n_ucv:]),
           w_glu.astype(BF16), row(b_glu),
           w_cout.astype(BF16), row(b_cout), w_out.astype(BF16), row(b_out),
           row(ln1_g), row(ln1_b), w_r_hi, w_r_lo, b_r)
    ht, routet = _mix(x2, yst, c_act, wts, rg, alpha)

    tm_moe = 128
    cls = routet[:, :, 4].reshape(n_tok).astype(jnp.int32)
    blk_ea, blk_eb, blk_used, tok_blocks, pos = _dispatch(cls, tm_moe)
    w_gu = jnp.concatenate([w_gate, w_up], axis=-1).astype(BF16)
    y_buf = _moe(blk_ea, blk_eb, blk_used, tok_blocks, ht.reshape(n_tok * TOKEN_SUBLANES, LANES), w_gu,
                 w_down.astype(BF16), tm_moe)
    tt = CHUNK * CHUNK
    n_r = tt // CHUNK
    pos_blocks = pos.reshape(CHUNK, n_rows // n_r, n_r).transpose(1, 0, 2).reshape(n_rows // n_r, 1, tt)
    out = _combine(pos_blocks, ht, routet, y_buf, row(ln2_g), row(ln2_b), tt, alpha)
    return out.reshape(bsz, seq, D_MODEL)


def kernel(x, w_in, b_in, lam_re, lam_im, log_dt, ssm_b_re, ssm_b_im, ssm_c_re, ssm_c_im, ssm_d, w_glu, b_glu, conv_w, conv_b, ln_c_g, ln_c_b, w_cout, b_cout, w_out, b_out, ln1_g, ln1_b, w_route_group, b_route_group, w_route_expert, b_route_expert, w_gate, w_up, w_down, ln2_g, ln2_b):
    params = (w_in, b_in, lam_re, lam_im, log_dt, ssm_b_re, ssm_b_im, ssm_c_re, ssm_c_im,
              ssm_d, w_glu, b_glu, conv_w, conv_b, ln_c_g, ln_c_b, w_cout, b_cout, w_out, b_out,
              ln1_g, ln1_b, w_route_group, b_route_group, w_route_expert, b_route_expert,
              w_gate, w_up, w_down, ln2_g, ln2_b)
    depth = w_in.shape[0]
    alpha = (2.0 * depth) ** 0.25
    h = x
    for i in range(depth):
        h = _layer(h, tuple(p[i] for p in params), alpha)
    return h
```

```python
import functools
import math

import jax
import jax.numpy as jnp
from jax import lax
from jax.experimental import pallas as pl
from jax.experimental.pallas import tpu as pltpu

F32 = jnp.float32
BF16 = jnp.bfloat16
HIGHEST = lax.Precision.HIGHEST

D_MODEL = 1024
D_SSM = 512
SSM_GROUP = 16
N_SSM_GROUPS = D_SSM // SSM_GROUP
SSM_STATE = 64
D_CONV = 512
CONV_WIDTH = 31
N_GROUPS = 4
N_EXP_PER_GROUP = 8
N_EXPERTS = N_GROUPS * N_EXP_PER_GROUP
TOP_K_INNER = 2
N_PAIRS = N_EXP_PER_GROUP * (N_EXP_PER_GROUP - 1) // 2
N_CLASSES = N_GROUPS * N_PAIRS
D_FF_EXPERT = D_MODEL // 4
LN_EPS = 1e-5

CHUNK = 16
CHUNK_F = CHUNK * SSM_GROUP
STATE_F = 2 * SSM_STATE
ROUTE_LANES = 128
HALO = 32
HALO_CHUNKS = 8
SUBLANES = 8
LANES = 128
TOKEN_SUBLANES = D_MODEL // LANES
PAIR_SUBLANES = TOP_K_INNER * TOKEN_SUBLANES
GATHER_SLOTS = 3
ROW_GROUP = 8
VMEM_LIMIT = 56 * 1024 * 1024

_NT = (((1,), (1,)), ((), ()))
_TN = (((0,), (0,)), ((), ()))


def _layer_norm(x, g, b):
    mu = jnp.mean(x, axis=-1, keepdims=True)
    xc = x - mu
    var = jnp.mean(xc * xc, axis=-1, keepdims=True)
    return xc * lax.rsqrt(var + LN_EPS) * g + b


def _inproj_kernel(x_ref, wut_ref, but_ref, wcv_ref, bcv_ref, ut_ref, v_ref):
    x = x_ref[...].astype(BF16)
    ut = lax.dot_general(wut_ref[...], x, _NT, preferred_element_type=F32) + but_ref[...]
    ut_ref[...] = ut.astype(BF16)
    p = jnp.dot(x, wcv_ref[...], preferred_element_type=F32) + bcv_ref[...]
    v_ref[...] = p[:, :D_CONV] * jax.nn.sigmoid(p[:, D_CONV:])


def _inproj(x2, w_ut, b_ut, w_cv, b_cv, rg):
    n_rows = x2.shape[0]
    const = lambda i, t: (0, 0)
    return pl.pallas_call(
        _inproj_kernel,
        grid=(n_rows // rg, CHUNK),
        in_specs=[
            pl.BlockSpec((rg, D_MODEL), lambda i, t: (i, t)),
            pl.BlockSpec(w_ut.shape, const),
            pl.BlockSpec(b_ut.shape, const),
            pl.BlockSpec(w_cv.shape, const),
            pl.BlockSpec(b_cv.shape, const),
        ],
        out_specs=[
            pl.BlockSpec((None, D_SSM, rg), lambda i, t: (t, 0, i)),
            pl.BlockSpec((None, rg, D_CONV), lambda i, t: (t, i, 0)),
        ],
        out_shape=[
            jax.ShapeDtypeStruct((CHUNK, D_SSM, n_rows), BF16),
            jax.ShapeDtypeStruct((CHUNK, n_rows, D_CONV), F32),
        ],
        compiler_params=pltpu.CompilerParams(
            dimension_semantics=("parallel", "arbitrary"), vmem_limit_bytes=VMEM_LIMIT),
        name="inproj",
    )(x2, w_ut, b_ut, w_cv, b_cv)


def _s5_chunk_weights(lam_re, lam_im, log_dt, b_re, b_im, c_re, c_im, n_steps):
    f32 = F32
    lr, li = lam_re.astype(f32), lam_im.astype(f32)
    dt = jnp.exp(log_dt.astype(f32))[:, None]
    mag = jnp.exp(lr * dt)
    ar, ai = mag * jnp.cos(li * dt), mag * jnp.sin(li * dt)
    den = lr * lr + li * li
    fr = ((ar - 1.0) * lr + ai * li) / den
    fi = (ai * lr - (ar - 1.0) * li) / den
    br, bi = b_re.astype(f32), b_im.astype(f32)
    bbr = fr[..., None] * br - fi[..., None] * bi
    bbi = fr[..., None] * bi + fi[..., None] * br
    cr, ci = c_re.astype(f32), c_im.astype(f32)

    kk_pow = jnp.arange(CHUNK + 1, dtype=f32)[:, None, None]
    mag_k = jnp.exp(kk_pow * (lr * dt)[None])
    pw_r = mag_k * jnp.cos(kk_pow * (li * dt)[None])
    pw_i = mag_k * jnp.sin(kk_pow * (li * dt)[None])

    ca_r = cr[None] * pw_r[:, :, None, :] - ci[None] * pw_i[:, :, None, :]
    ca_i = cr[None] * pw_i[:, :, None, :] + ci[None] * pw_r[:, :, None, :]
    kk = (jnp.einsum('kghp,gpq->kghq', ca_r[:CHUNK], bbr, precision=HIGHEST)
          - jnp.einsum('kghp,gpq->kghq', ca_i[:CHUNK], bbi, precision=HIGHEST))
    t_in = jnp.arange(CHUNK)[:, None]
    t_out = jnp.arange(CHUNK)[None, :]
    lag = t_out - t_in
    m = kk[jnp.maximum(lag, 0)]
    m = jnp.where((lag >= 0)[:, :, None, None, None], m, 0.0)
    m = m.transpose(2, 1, 3, 0, 4).reshape(N_SSM_GROUPS, CHUNK_F, CHUNK_F)

    rp_r, rp_i = pw_r[CHUNK - 1::-1][:CHUNK], pw_i[CHUNK - 1::-1][:CHUNK]
    z_r = rp_r[..., None] * bbr[None] - rp_i[..., None] * bbi[None]
    z_i = rp_r[..., None] * bbi[None] + rp_i[..., None] * bbr[None]
    z = jnp.concatenate([z_r, z_i], axis=2)
    z = z.transpose(1, 2, 0, 3).reshape(N_SSM_GROUPS, STATE_F, CHUNK_F)
    mzt = jnp.concatenate([m, z], axis=1)

    w = jnp.concatenate([ca_r[1:], -ca_i[1:]], axis=-1)
    wt = w.transpose(1, 0, 2, 3).reshape(N_SSM_GROUPS, CHUNK_F, STATE_F)

    sr, si = pw_r[CHUNK], pw_i[CHUNK]
    c1, c2 = [], []
    for _ in range(n_steps):
        c1.append(jnp.concatenate([sr, sr], axis=-1)[None])
        c2.append(jnp.concatenate([-si, si], axis=-1)[None])
        sr, si = sr * sr - si * si, 2.0 * sr * si
    pad = jnp.zeros((8 - n_steps, N_SSM_GROUPS, STATE_F), f32)
    coef = jnp.concatenate(c1 + [pad] + c2 + [pad], axis=0).transpose(1, 2, 0)
    return mzt.astype(BF16), wt.astype(BF16), coef


def _s5_kernel(ut_ref, mzt_ref, wt_ref, coef_ref, dcol_ref, o_ref, *, nj, n_steps):
    ut = jnp.concatenate([ut_ref[t] for t in range(CHUNK)], axis=0)
    r = jnp.dot(mzt_ref[0], ut, preferred_element_type=F32)
    y0 = r[:CHUNK_F]
    t = r[CHUNK_F:]
    j = lax.broadcasted_iota(jnp.int32, t.shape, 1) & (nj - 1)
    for m in range(n_steps):
        d = 1 << m
        sh = jnp.where(j >= d, pltpu.roll(t, d, axis=1), 0.0)
        p1 = coef_ref[0, :, m:m + 1]
        p2 = coef_ref[0, :, 8 + m:9 + m]
        t = t + p1 * sh + p2 * pltpu.roll(sh, SSM_STATE, axis=0)
    s_prev = jnp.where(j >= 1, pltpu.roll(t, 1, axis=1), 0.0)
    y = y0 + jnp.dot(wt_ref[0], s_prev.astype(BF16), preferred_element_type=F32)
    ys = jax.nn.gelu(y + dcol_ref[0] * ut.astype(F32)).astype(BF16)
    for t_out in range(CHUNK):
        o_ref[t_out] = ys[t_out * SSM_GROUP:(t_out + 1) * SSM_GROUP]


def _s5(ut4, mzt, wt, coef, dcol, nj, n_steps):
    _, g, _, n_rows = ut4.shape
    blk = pl.BlockSpec((CHUNK, None, SSM_GROUP, n_rows), lambda i: (0, i, 0, 0))
    return pl.pallas_call(
        functools.partial(_s5_kernel, nj=nj, n_steps=n_steps),
        grid=(g,),
        in_specs=[
            blk,
            pl.BlockSpec((1, CHUNK_F + STATE_F, CHUNK_F), lambda i: (i, 0, 0)),
            pl.BlockSpec((1, CHUNK_F, STATE_F), lambda i: (i, 0, 0)),
            pl.BlockSpec((1, STATE_F, 16), lambda i: (i, 0, 0)),
            pl.BlockSpec((1, CHUNK_F, 1), lambda i: (i, 0, 0)),
        ],
        out_specs=blk,
        out_shape=jax.ShapeDtypeStruct(ut4.shape, BF16),
        compiler_params=pltpu.CompilerParams(
            dimension_semantics=("parallel",), vmem_limit_bytes=VMEM_LIMIT),
        name="s5",
    )(ut4, mzt, wt, coef, dcol)


def _conv_kernel(v_ref, vh_ref, cw_ref, cb_ref, g_ref, b_ref, c_ref, vext_ref, shift_ref, conv_ref, *, tl, sub):
    i = pl.program_id(1)
    n_ext = tl + HALO
    n_chunks = tl // CHUNK
    n_hist = HALO // CHUNK
    blocks = [slice(c * LANES, (c + 1) * LANES) for c in range(D_CONV // LANES)]
    for t in range(CHUNK):
        for c, lanes in enumerate(blocks):
            hist = vh_ref[t, HALO_CHUNKS - n_hist:HALO_CHUNKS, lanes]
            vext_ref[c, pl.ds(t, n_hist, stride=CHUNK), :] = jnp.where(i > 0, hist, 0.0)
            vext_ref[c, pl.ds(HALO + t, n_chunks, stride=CHUNK), :] = v_ref[t, :, lanes]
    for m in range(SUBLANES):
        shift_ref[m, :, 0:n_ext - m, :] = vext_ref[:, m:n_ext, :]
    off = HALO - (CONV_WIDTH - 1)
    for c, lanes in enumerate(blocks):
        for s in range(tl // sub):
            acc = jnp.zeros((sub, LANES), F32) + cb_ref[:, lanes]
            for k in range(CONV_WIDTH):
                q, m = divmod(off + k, SUBLANES)
                r0 = s * sub + q * SUBLANES
                acc = acc + cw_ref[k:k + 1, lanes] * shift_ref[m, c, r0:r0 + sub, :]
            conv_ref[c, s * sub:(s + 1) * sub, :] = acc
    xs = [conv_ref[c] for c in range(len(blocks))]
    mu = sum(jnp.sum(x, axis=-1, keepdims=True) for x in xs) * (1.0 / D_CONV)
    xc = [x - mu for x in xs]
    var = sum(jnp.sum(x * x, axis=-1, keepdims=True) for x in xc) * (1.0 / D_CONV)
    inv = lax.rsqrt(var + LN_EPS)
    for c, lanes in enumerate(blocks):
        conv_ref[c] = jax.nn.silu(xc[c] * inv * g_ref[:, lanes] + b_ref[:, lanes])
    for t in range(CHUNK):
        for c, lanes in enumerate(blocks):
            c_ref[t, :, lanes] = conv_ref[c, pl.ds(t, n_chunks, stride=CHUNK), :].astype(BF16)


def _conv(vt, b, l, conv_w, conv_b, ln_g, ln_b, tl):
    sub = min(64, tl)
    n_chunks = tl // CHUNK
    assert n_chunks % HALO_CHUNKS == 0
    tiles = l // tl
    const = lambda bi, i: (0, 0)
    row_spec = pl.BlockSpec((CHUNK, n_chunks, D_CONV), lambda bi, i: (0, bi * tiles + i, 0))
    halo_spec = pl.BlockSpec(
        (CHUNK, HALO_CHUNKS, D_CONV),
        lambda bi, i: (0, jnp.maximum((bi * tiles + i) * (n_chunks // HALO_CHUNKS) - 1, 0), 0))
    return pl.pallas_call(
        functools.partial(_conv_kernel, tl=tl, sub=sub),
        grid=(b, l // tl),
        in_specs=[row_spec, halo_spec,
                  pl.BlockSpec(conv_w.shape, const), pl.BlockSpec(conv_b.shape, const),
                  pl.BlockSpec(ln_g.shape, const), pl.BlockSpec(ln_b.shape, const)],
        out_specs=row_spec,
        out_shape=jax.ShapeDtypeStruct((CHUNK, b * l // CHUNK, D_CONV), BF16),
        scratch_shapes=[
            pltpu.VMEM((D_CONV // LANES, tl + HALO, LANES), F32),
            pltpu.VMEM((SUBLANES, D_CONV // LANES, tl + HALO, LANES), F32),
            pltpu.VMEM((D_CONV // LANES, tl, LANES), F32),
        ],
        compiler_params=pltpu.CompilerParams(
            dimension_semantics=("parallel", "arbitrary"), vmem_limit_bytes=VMEM_LIMIT),
        name="conv",
    )(vt, vt, conv_w, conv_b, ln_g, ln_b)


def _mix_kernel(x_ref, yst_ref, c_ref,
                wg_ref, bg_ref, wglu_ref, bglu_ref, wco_ref, bco_ref,
                wo_ref, bo_ref, l1g_ref, l1b_ref, wrh_ref, wrl_ref, br_ref,
                h_ref, route_ref, *, alpha):
    c_out = jnp.dot(c_ref[...], wco_ref[...], preferred_element_type=F32) + bco_ref[...]

    glu = lax.dot_general(yst_ref[...], wglu_ref[...], _TN, preferred_element_type=F32) + bglu_ref[...]
    s_out = glu[:, :D_MODEL] * jax.nn.sigmoid(glu[:, D_MODEL:])

    x = x_ref[...]
    gates = jnp.dot(x.astype(BF16), wg_ref[...], preferred_element_type=F32) + bg_ref[...]
    merged = jax.nn.sigmoid(gates[:, :D_MODEL]) * s_out + jax.nn.sigmoid(gates[:, D_MODEL:]) * c_out
    mix = jnp.dot(merged.astype(BF16), wo_ref[...], preferred_element_type=F32) + bo_ref[...]
    h = _layer_norm(alpha * x + mix, l1g_ref[...], l1b_ref[...])
    rg = h.shape[0]
    for s in range(TOKEN_SUBLANES):
        h_ref[pl.ds(s, rg, stride=TOKEN_SUBLANES), :] = h[:, s * LANES:(s + 1) * LANES]

    h_hi = h.astype(BF16)
    h_lo = (h - h_hi.astype(F32)).astype(BF16)
    z = (jnp.dot(h_hi, wrh_ref[...], preferred_element_type=F32)
         + jnp.dot(h_lo, wrh_ref[...], preferred_element_type=F32)
         + jnp.dot(h_hi, wrl_ref[...], preferred_element_type=F32)) + br_ref[...]
    lane = lax.broadcasted_iota(jnp.int32, z.shape, 1)
    lanef = lane.astype(F32)
    neg = jnp.float32(-jnp.inf)
    big = jnp.float32(1e9)
    gmask = lane < N_GROUPS
    zg = jnp.where(gmask, z, neg)
    gmax = jnp.max(zg, axis=1, keepdims=True)
    gsel = jnp.min(jnp.where(gmask & (zg == gmax), lanef, big), axis=1, keepdims=True)
    p_group = 1.0 / jnp.sum(jnp.where(gmask, jnp.exp(zg - gmax), 0.0), axis=1, keepdims=True)
    lo = N_GROUPS + N_EXP_PER_GROUP * gsel
    emask = (lanef >= lo) & (lanef < lo + N_EXP_PER_GROUP)
    ze = jnp.where(emask, z, neg)
    v1 = jnp.max(ze, axis=1, keepdims=True)
    i1 = jnp.min(jnp.where(emask & (ze == v1), lanef, big), axis=1, keepdims=True)
    emask2 = emask & (lanef != i1)
    ze2 = jnp.where(emask2, z, neg)
    v2 = jnp.max(ze2, axis=1, keepdims=True)
    i2 = jnp.min(jnp.where(emask2 & (ze2 == v2), lanef, big), axis=1, keepdims=True)
    e2 = jnp.exp(v2 - v1)
    inv = 1.0 / (1.0 + e2)
    g1 = p_group * inv
    g2 = p_group * (e2 * inv)
    a = jnp.minimum(i1, i2) - lo
    b = jnp.maximum(i1, i2) - lo
    cls = gsel * N_PAIRS + (a * (2 * N_EXP_PER_GROUP - 1 - a)) * 0.5 + (b - a - 1.0)
    out = jnp.where(lane == 0, i1 - N_GROUPS, 0.0)
    out = jnp.where(lane == 1, i2 - N_GROUPS, out)
    out = jnp.where(lane == 2, g1, out)
    out = jnp.where(lane == 3, g2, out)
    out = jnp.where(lane == 4, cls, out)
    route_ref[...] = out


def _mix(x2, yst, ct, wts, rg, alpha):
    n_rows = x2.shape[0]

    def tau_spec(f):
        return pl.BlockSpec((None, rg, f), lambda i, t: (t, i, 0))

    def full_spec(a):
        nd = a.ndim
        return pl.BlockSpec(a.shape, lambda i, t: (0,) * nd)

    return pl.pallas_call(
        functools.partial(_mix_kernel, alpha=alpha),
        grid=(n_rows // rg, CHUNK),
        in_specs=[pl.BlockSpec((rg, D_MODEL), lambda i, t: (i, t)),
                  pl.BlockSpec((None, D_SSM, rg), lambda i, t: (t, 0, i)),
                  tau_spec(D_CONV)]
                 + [full_spec(a) for a in wts],
        out_specs=[pl.BlockSpec((None, rg * TOKEN_SUBLANES, LANES), lambda i, t: (t, i, 0)),
                   tau_spec(ROUTE_LANES)],
        out_shape=[
            jax.ShapeDtypeStruct((CHUNK, n_rows * TOKEN_SUBLANES, LANES), F32),
            jax.ShapeDtypeStruct((CHUNK, n_rows, ROUTE_LANES), F32),
        ],
        compiler_params=pltpu.CompilerParams(
            dimension_semantics=("parallel", "arbitrary"), vmem_limit_bytes=VMEM_LIMIT),
        name="mix",
    )(x2, yst, ct, *wts)


def _gather_rows(idx_ref, base, src_hbm, dst, sem, n, rows_per):
    def issue(r, carry):
        src0 = pl.multiple_of(idx_ref[base + r] * rows_per, rows_per)
        dst0 = pl.multiple_of(r * rows_per, rows_per)
        pltpu.make_async_copy(src_hbm.at[pl.ds(src0, rows_per)], dst.at[pl.ds(dst0, rows_per)], sem).start()
        return carry
    lax.fori_loop(0, n, issue, 0, unroll=8)


def _gather_row_groups(idx_ref, base, src_hbm, dst, sem, n_groups, rows_per):
    def issue(g, carry):
        for k in range(ROW_GROUP):
            r = g * ROW_GROUP + k
            src0 = pl.multiple_of(idx_ref[base + r] * rows_per, rows_per)
            dst0 = pl.multiple_of(r * rows_per, rows_per)
            pltpu.make_async_copy(src_hbm.at[pl.ds(src0, rows_per)], dst.at[pl.ds(dst0, rows_per)], sem).start()
        return carry
    lax.fori_loop(0, n_groups, issue, 0)


def _wait_row_groups(src_hbm, dst, sem, n_groups, rows_per):
    rows = ROW_GROUP * rows_per

    def wait(g, carry):
        pltpu.make_async_copy(src_hbm.at[pl.ds(0, rows)], dst.at[pl.ds(0, rows)], sem).wait()
        return carry
    lax.fori_loop(0, n_groups, wait, 0)


def _wait_rows(src_hbm, dst, sem):
    pltpu.make_async_copy(src_hbm.at[pl.ds(0, dst.shape[0])], dst, sem).wait()


def _lane_blocks(ref2d, n, rows_per):
    return jnp.concatenate([ref2d[pl.ds(s, n, stride=rows_per), :] for s in range(rows_per)], axis=1)


def _expert(xb, wgu, wd):
    gu = jnp.dot(xb, wgu, preferred_element_type=F32)
    hid = jax.nn.silu(gu[:, :D_FF_EXPERT]) * gu[:, D_FF_EXPERT:]
    return jnp.dot(hid.astype(BF16), wd, preferred_element_type=F32)


def _moe_kernel(ea_ref, eb_ref, groups_ref, tok_ref, h_hbm, wgu_a_ref, wgu_b_ref, wd_a_ref, wd_b_ref,
                y_ref, xbuf, sem, *, tm):
    del ea_ref, eb_ref
    i = pl.program_id(0)
    n = pl.num_programs(0)
    slot = i % GATHER_SLOTS

    def gather(blk):
        n_groups = groups_ref[jnp.minimum(blk, n - 1)]

        @pl.when((blk < n) & (n_groups > 0))
        def _():
            s = blk % GATHER_SLOTS
            _gather_row_groups(tok_ref, blk * tm, h_hbm, xbuf.at[s], sem.at[s], n_groups, TOKEN_SUBLANES)

    @pl.when(i == 0)
    def _():
        xbuf[...] = jnp.zeros_like(xbuf)
        for blk in range(GATHER_SLOTS - 1):
            gather(jnp.int32(blk))

    @pl.when(groups_ref[i] > 0)
    def _():
        _wait_row_groups(h_hbm, xbuf.at[slot], sem.at[slot], groups_ref[i], TOKEN_SUBLANES)
        xb = _lane_blocks(xbuf.at[slot], tm, TOKEN_SUBLANES).astype(BF16)
        ya = _expert(xb, wgu_a_ref[0], wd_a_ref[0])
        yb = _expert(xb, wgu_b_ref[0], wd_b_ref[0])
        for k, y in enumerate((ya, yb)):
            for s in range(TOKEN_SUBLANES):
                y_ref[pl.ds(k * TOKEN_SUBLANES + s, tm, stride=PAIR_SUBLANES), :] = y[:, s * LANES:(s + 1) * LANES]

    @pl.when(groups_ref[i] == 0)
    def _():
        y_ref[...] = jnp.zeros_like(y_ref)

    gather(i + GATHER_SLOTS - 1)


def _moe(blk_ea, blk_eb, blk_groups, tok_rows, h2, w_gu, w_d, tm):
    n_blocks = blk_ea.shape[0]
    grid_spec = pltpu.PrefetchScalarGridSpec(
        num_scalar_prefetch=4,
        grid=(n_blocks,),
        in_specs=[
            pl.BlockSpec(memory_space=pl.ANY),
            pl.BlockSpec((1, D_MODEL, 2 * D_FF_EXPERT), lambda i, ea, eb, groups, tok: (ea[i], 0, 0)),
            pl.BlockSpec((1, D_MODEL, 2 * D_FF_EXPERT), lambda i, ea, eb, groups, tok: (eb[i], 0, 0)),
            pl.BlockSpec((1, D_FF_EXPERT, D_MODEL), lambda i, ea, eb, groups, tok: (ea[i], 0, 0)),
            pl.BlockSpec((1, D_FF_EXPERT, D_MODEL), lambda i, ea, eb, groups, tok: (eb[i], 0, 0)),
        ],
        out_specs=pl.BlockSpec((tm * PAIR_SUBLANES, LANES), lambda i, ea, eb, groups, tok: (i, 0)),
        scratch_shapes=[pltpu.VMEM((GATHER_SLOTS, tm * TOKEN_SUBLANES, LANES), F32),
                        pltpu.SemaphoreType.DMA((GATHER_SLOTS,))],
    )
    return pl.pallas_call(
        functools.partial(_moe_kernel, tm=tm),
        grid_spec=grid_spec,
        out_shape=jax.ShapeDtypeStruct((n_blocks * tm * PAIR_SUBLANES, LANES), F32),
        compiler_params=pltpu.CompilerParams(
            dimension_semantics=("arbitrary",), vmem_limit_bytes=VMEM_LIMIT),
        name="moe",
    )(blk_ea, blk_eb, blk_groups, tok_rows, h2, w_gu, w_gu, w_d, w_d)


def _combine_kernel(pos_ref, h_ref, route_ref, y_hbm, g_ref, b_ref, o_ref, buf, res_ref, sem, *, tt, alpha):
    i = pl.program_id(0)
    n = pl.num_programs(0)
    slot = i % 2
    n_r = tt // CHUNK

    @pl.when(i == 0)
    def _():
        _gather_rows(pos_ref, 0, y_hbm, buf.at[0], sem.at[0], tt, PAIR_SUBLANES)

    @pl.when(i + 1 < n)
    def _():
        _gather_rows(pos_ref, (i + 1) * tt, y_hbm, buf.at[1 - slot], sem.at[1 - slot], tt, PAIR_SUBLANES)

    _wait_rows(y_hbm, buf.at[slot], sem.at[slot])
    y2 = _lane_blocks(buf.at[slot], tt, PAIR_SUBLANES)
    h = jnp.concatenate([_lane_blocks(h_ref.at[t], n_r, TOKEN_SUBLANES) for t in range(CHUNK)], axis=0)
    route = route_ref[...].reshape(tt, ROUTE_LANES)
    first_is_a = route[:, 0:1] < route[:, 1:2]
    g1 = route[:, 2:3]
    g2 = route[:, 3:4]
    wa = jnp.where(first_is_a, g1, g2)
    wb = jnp.where(first_is_a, g2, g1)
    moe = wa * y2[:, :D_MODEL] + wb * y2[:, D_MODEL:]
    res = _layer_norm(alpha * h + moe, g_ref[...], b_ref[...])
    n_cb = D_MODEL // LANES
    for c in range(n_cb):
        res_ref[c] = res[:, c * LANES:(c + 1) * LANES]
    for r in range(n_r):
        for c in range(n_cb):
            o_ref[r * CHUNK:(r + 1) * CHUNK, c * LANES:(c + 1) * LANES] = res_ref[c, pl.ds(r, CHUNK, stride=n_r), :]


def _combine(pos_rows, ht, routet, y_buf, ln_g, ln_b, tt, alpha):
    n_tok = routet.shape[0] * routet.shape[1]
    n_tiles = n_tok // tt
    n_r = tt // CHUNK
    grid_spec = pltpu.PrefetchScalarGridSpec(
        num_scalar_prefetch=1,
        grid=(n_tiles,),
        in_specs=[
            pl.BlockSpec((CHUNK, n_r * TOKEN_SUBLANES, LANES), lambda i, pos: (0, i, 0)),
            pl.BlockSpec((CHUNK, n_r, ROUTE_LANES), lambda i, pos: (0, i, 0)),
            pl.BlockSpec(memory_space=pl.ANY),
            pl.BlockSpec((1, D_MODEL), lambda i, pos: (0, 0)),
            pl.BlockSpec((1, D_MODEL), lambda i, pos: (0, 0)),
        ],
        out_specs=pl.BlockSpec((tt, D_MODEL), lambda i, pos: (i, 0)),
        scratch_shapes=[
            pltpu.VMEM((2, tt * PAIR_SUBLANES, LANES), F32),
            pltpu.VMEM((D_MODEL // LANES, tt, LANES), F32),
            pltpu.SemaphoreType.DMA((2,)),
        ],
    )
    return pl.pallas_call(
        functools.partial(_combine_kernel, tt=tt, alpha=alpha),
        grid_spec=grid_spec,
        out_shape=jax.ShapeDtypeStruct((n_tok, D_MODEL), F32),
        compiler_params=pltpu.CompilerParams(
            dimension_semantics=("arbitrary",), vmem_limit_bytes=VMEM_LIMIT),
        name="combine",
    )(pos_rows, ht, routet, y_buf, ln_g, ln_b)


def _class_experts():
    ea, eb = [], []
    for g in range(N_GROUPS):
        for a in range(N_EXP_PER_GROUP):
            for b in range(a + 1, N_EXP_PER_GROUP):
                ea.append(g * N_EXP_PER_GROUP + a)
                eb.append(g * N_EXP_PER_GROUP + b)
    return jnp.asarray(ea, jnp.int32), jnp.asarray(eb, jnp.int32)


def _lookup(table, idx):
    n = table.shape[0]
    hit = idx[..., None] == jnp.arange(n, dtype=idx.dtype)
    return jnp.sum(jnp.where(hit, table, 0), axis=-1).astype(table.dtype)


def _dispatch(cls, tm):
    n_tok = cls.shape[0]
    i32 = jnp.int32
    order = jnp.argsort(cls).astype(i32)
    inv = jnp.argsort(order).astype(i32)
    counts = jnp.sum(cls[:, None] == jnp.arange(N_CLASSES, dtype=i32)[None, :], axis=0).astype(i32)
    start = jnp.cumsum(counts) - counts
    pcounts = (counts + tm - 1) // tm * tm
    pend = jnp.cumsum(pcounts)
    pstart = pend - pcounts
    n_blocks = -(-n_tok // tm) + N_CLASSES
    blk_start = jnp.arange(n_blocks, dtype=i32) * tm
    blk_cls = jnp.minimum(jnp.sum(blk_start[:, None] >= pend[None, :], axis=1), N_CLASSES - 1).astype(i32)
    cls_ea, cls_eb = _class_experts()
    blk_first = blk_start - _lookup(pstart, blk_cls)
    blk_count = _lookup(counts, blk_cls)
    blk_groups = ((jnp.clip(blk_count - blk_first, 0, tm) + ROW_GROUP - 1) // ROW_GROUP).astype(i32)
    within = blk_first[:, None] + jnp.arange(tm, dtype=i32)[None, :]
    valid = within < blk_count[:, None]
    src = jnp.clip(_lookup(start, blk_cls)[:, None] + within, 0, n_tok - 1)
    tok_buf = jnp.where(valid, order[src.reshape(-1)].reshape(n_blocks, tm), 0).astype(i32)
    pos = (_lookup(pstart - start, cls) + inv).astype(i32)
    return _lookup(cls_ea, blk_cls), _lookup(cls_eb, blk_cls), blk_groups, tok_buf.reshape(-1), pos


def _layer(x, p, alpha):
    (w_in, b_in, lam_re, lam_im, log_dt, ssm_b_re, ssm_b_im, ssm_c_re, ssm_c_im,
     ssm_d, w_glu, b_glu, conv_w, conv_b, ln_c_g, ln_c_b, w_cout, b_cout, w_out, b_out,
     ln1_g, ln1_b, w_rg, b_rg, w_re, b_re, w_gate, w_up, w_down, ln2_g, ln2_b) = p
    bsz, seq, _ = x.shape
    n_tok = bsz * seq
    assert seq % CHUNK == 0
    nj = seq // CHUNK
    n_rows = bsz * nj
    assert nj & (nj - 1) == 0, "chunks per sequence must be a power of two"
    n_steps = max(1, int(math.log2(nj)))
    assert n_steps <= 8

    n_ucv = D_SSM + 2 * D_CONV
    row = lambda a: a.reshape(1, -1).astype(F32)
    rg = min(512, n_rows)

    x2 = x.reshape(n_rows, CHUNK * D_MODEL)
    ut, vt = _inproj(x2, w_in[:, :D_SSM].T.astype(BF16), b_in[:D_SSM].reshape(-1, 1).astype(F32),
                     w_in[:, D_SSM:n_ucv].astype(BF16), row(b_in[D_SSM:n_ucv]), rg)

    mzt, wt_state, coef = _s5_chunk_weights(lam_re, lam_im, log_dt, ssm_b_re, ssm_b_im,
                                            ssm_c_re, ssm_c_im, n_steps)
    dcol = jnp.tile(ssm_d.astype(F32).reshape(N_SSM_GROUPS, 1, SSM_GROUP), (1, CHUNK, 1))
    dcol = dcol.reshape(N_SSM_GROUPS, CHUNK_F, 1)
    yst = _s5(ut.reshape(CHUNK, N_SSM_GROUPS, SSM_GROUP, n_rows), mzt, wt_state, coef, dcol, nj, n_steps)
    yst = yst.reshape(CHUNK, D_SSM, n_rows)

    tl = min(512, seq)
    c_act = _conv(vt, bsz, seq, conv_w.reshape(CONV_WIDTH, D_CONV).astype(F32),
                  row(conv_b), row(ln_c_g), row(ln_c_b), tl)

    w_r = jnp.zeros((D_MODEL, ROUTE_LANES), F32)
    w_r = w_r.at[:, :N_GROUPS].set(w_rg).at[:, N_GROUPS:N_GROUPS + N_EXPERTS].set(w_re)
    b_r = jnp.zeros((1, ROUTE_LANES), F32)
    b_r = b_r.at[0, :N_GROUPS].set(b_rg).at[0, N_GROUPS:N_GROUPS + N_EXPERTS].set(b_re)
    w_r_hi = w_r.astype(BF16)
    w_r_lo = (w_r - w_r_hi.astype(F32)).astype(BF16)
    wts = (w_in[:, n_ucv:].astype(BF16), row(b_in[n_ucv:]),
           w_glu.astype(BF16), row(b_glu),
           w_cout.astype(BF16), row(b_cout), w_out.astype(BF16), row(b_out),
           row(ln1_g), row(ln1_b), w_r_hi, w_r_lo, b_r)
    ht, routet = _mix(x2, yst, c_act, wts, rg, alpha)

    tm_moe = 128
    cls = routet[:, :, 4].reshape(n_tok).astype(jnp.int32)
    blk_ea, blk_eb, blk_groups, tok_rows, pos = _dispatch(cls, tm_moe)
    w_gu = jnp.concatenate([w_gate, w_up], axis=-1).astype(BF16)
    y_buf = _moe(blk_ea, blk_eb, blk_groups, tok_rows, ht.reshape(n_tok * TOKEN_SUBLANES, LANES), w_gu,
                 w_down.astype(BF16), tm_moe)
    tt = CHUNK * CHUNK
    n_r = tt // CHUNK
    pos_rows = pos.reshape(CHUNK, n_rows // n_r, n_r).transpose(1, 0, 2).reshape(n_tok)
    out = _combine(pos_rows, ht, routet, y_buf, row(ln2_g), row(ln2_b), tt, alpha)
    return out.reshape(bsz, seq, D_MODEL)


def kernel(x, w_in, b_in, lam_re, lam_im, log_dt, ssm_b_re, ssm_b_im, ssm_c_re, ssm_c_im, ssm_d, w_glu, b_glu, conv_w, conv_b, ln_c_g, ln_c_b, w_cout, b_cout, w_out, b_out, ln1_g, ln1_b, w_route_group, b_route_group, w_route_expert, b_route_expert, w_gate, w_up, w_down, ln2_g, ln2_b):
    params = (w_in, b_in, lam_re, lam_im, log_dt, ssm_b_re, ssm_b_im, ssm_c_re, ssm_c_im,
              ssm_d, w_glu, b_glu, conv_w, conv_b, ln_c_g, ln_c_b, w_cout, b_cout, w_out, b_out,
              ln1_g, ln1_b, w_route_group, b_route_group, w_route_expert, b_route_expert,
              w_gate, w_up, w_down, ln2_g, ln2_b)
    depth = w_in.shape[0]
    alpha = (2.0 * depth) ** 0.25
    h = x
    for i in range(depth):
        h = _layer(h, tuple(p[i] for p in params), alpha)
    return h
```

```python
import functools
import math

import jax
import jax.numpy as jnp
from jax import lax
from jax.experimental import pallas as pl
from jax.experimental.pallas import tpu as pltpu

F32 = jnp.float32
BF16 = jnp.bfloat16
HIGHEST = lax.Precision.HIGHEST

D_MODEL = 1024
D_SSM = 512
SSM_GROUP = 16
N_SSM_GROUPS = D_SSM // SSM_GROUP
SSM_STATE = 64
D_CONV = 512
CONV_WIDTH = 31
N_GROUPS = 4
N_EXP_PER_GROUP = 8
N_EXPERTS = N_GROUPS * N_EXP_PER_GROUP
TOP_K_INNER = 2
N_PAIRS = N_EXP_PER_GROUP * (N_EXP_PER_GROUP - 1) // 2
N_CLASSES = N_GROUPS * N_PAIRS
D_FF_EXPERT = D_MODEL // 4
LN_EPS = 1e-5

CHUNK = 16
CHUNK_F = CHUNK * SSM_GROUP
STATE_F = 2 * SSM_STATE
ROUTE_LANES = 128
HALO = 32
HALO_CHUNKS = 8
SUBLANES = 8
LANES = 128
TOKEN_SUBLANES = D_MODEL // LANES
PAIR_SUBLANES = TOP_K_INNER * TOKEN_SUBLANES
GATHER_SLOTS = 3
ROW_GROUP = 8
VMEM_LIMIT = 56 * 1024 * 1024

_NT = (((1,), (1,)), ((), ()))
_TN = (((0,), (0,)), ((), ()))


def _layer_norm(x, g, b):
    mu = jnp.mean(x, axis=-1, keepdims=True)
    xc = x - mu
    var = jnp.mean(xc * xc, axis=-1, keepdims=True)
    return xc * lax.rsqrt(var + LN_EPS) * g + b


def _inproj_kernel(x_ref, wut_ref, but_ref, wcv_ref, bcv_ref, ut_ref, v_ref):
    x = x_ref[...].astype(BF16)
    ut = lax.dot_general(wut_ref[...], x, _NT, preferred_element_type=F32) + but_ref[...]
    ut_ref[...] = ut.astype(BF16)
    p = jnp.dot(x, wcv_ref[...], preferred_element_type=F32) + bcv_ref[...]
    v_ref[...] = p[:, :D_CONV] * jax.nn.sigmoid(p[:, D_CONV:])


def _inproj(x2, w_ut, b_ut, w_cv, b_cv, rg):
    n_rows = x2.shape[0]
    const = lambda i, t: (0, 0)
    return pl.pallas_call(
        _inproj_kernel,
        grid=(n_rows // rg, CHUNK),
        in_specs=[
            pl.BlockSpec((rg, D_MODEL), lambda i, t: (i, t)),
            pl.BlockSpec(w_ut.shape, const),
            pl.BlockSpec(b_ut.shape, const),
            pl.BlockSpec(w_cv.shape, const),
            pl.BlockSpec(b_cv.shape, const),
        ],
        out_specs=[
            pl.BlockSpec((None, D_SSM, rg), lambda i, t: (t, 0, i)),
            pl.BlockSpec((None, rg, D_CONV), lambda i, t: (t, i, 0)),
        ],
        out_shape=[
            jax.ShapeDtypeStruct((CHUNK, D_SSM, n_rows), BF16),
            jax.ShapeDtypeStruct((CHUNK, n_rows, D_CONV), F32),
        ],
        compiler_params=pltpu.CompilerParams(
            dimension_semantics=("parallel", "arbitrary"), vmem_limit_bytes=VMEM_LIMIT),
        name="inproj",
    )(x2, w_ut, b_ut, w_cv, b_cv)


def _s5_prep_kernel(ca_ref, bb_ref, pc_ref, mzt_ref):
    bb = bb_ref[0]
    k_all = jnp.dot(ca_ref[0, :CHUNK_F, :], bb, precision=HIGHEST, preferred_element_type=F32)
    mzt_ref[0, :CHUNK_F, :] = jnp.zeros((CHUNK_F, CHUNK_F), F32)
    bb_sw = pltpu.roll(bb, SSM_STATE, axis=0)
    for t_in in range(CHUNK):
        cols = slice(t_in * SSM_GROUP, (t_in + 1) * SSM_GROUP)
        mzt_ref[0, t_in * SSM_GROUP:CHUNK_F, cols] = k_all[:(CHUNK - t_in) * SSM_GROUP, :]
        mzt_ref[0, CHUNK_F:, cols] = (pc_ref[0, :, t_in:t_in + 1] * bb
                                      + pc_ref[0, :, CHUNK + t_in:CHUNK + t_in + 1] * bb_sw)


def _s5_prep(ca, bb, pcols):
    g = ca.shape[0]
    return pl.pallas_call(
        _s5_prep_kernel,
        grid=(g,),
        in_specs=[pl.BlockSpec((1,) + ca.shape[1:], lambda i: (i, 0, 0)),
                  pl.BlockSpec((1,) + bb.shape[1:], lambda i: (i, 0, 0)),
                  pl.BlockSpec((1,) + pcols.shape[1:], lambda i: (i, 0, 0))],
        out_specs=pl.BlockSpec((1, CHUNK_F + STATE_F, CHUNK_F), lambda i: (i, 0, 0)),
        out_shape=jax.ShapeDtypeStruct((g, CHUNK_F + STATE_F, CHUNK_F), F32),
        compiler_params=pltpu.CompilerParams(dimension_semantics=("parallel",)),
        name="s5prep",
    )(ca, bb, pcols)


def _s5_chunk_weights(lam_re, lam_im, log_dt, b_re, b_im, c_re, c_im, n_steps):
    f32 = F32
    lr, li = lam_re.astype(f32), lam_im.astype(f32)
    dt = jnp.exp(log_dt.astype(f32))[:, None]
    mag = jnp.exp(lr * dt)
    ar, ai = mag * jnp.cos(li * dt), mag * jnp.sin(li * dt)
    den = lr * lr + li * li
    fr = ((ar - 1.0) * lr + ai * li) / den
    fi = (ai * lr - (ar - 1.0) * li) / den
    br, bi = b_re.astype(f32), b_im.astype(f32)
    bbr = fr[..., None] * br - fi[..., None] * bi
    bbi = fr[..., None] * bi + fi[..., None] * br
    cr, ci = c_re.astype(f32), c_im.astype(f32)

    kk_pow = jnp.arange(CHUNK + 1, dtype=f32)[:, None, None]
    mag_k = jnp.exp(kk_pow * (lr * dt)[None])
    pw_r = mag_k * jnp.cos(kk_pow * (li * dt)[None])
    pw_i = mag_k * jnp.sin(kk_pow * (li * dt)[None])

    ca_r = cr[None] * pw_r[:, :, None, :] - ci[None] * pw_i[:, :, None, :]
    ca_i = cr[None] * pw_i[:, :, None, :] + ci[None] * pw_r[:, :, None, :]
    ca = jnp.concatenate([ca_r, -ca_i], axis=-1).transpose(1, 0, 2, 3)
    ca = ca.reshape(N_SSM_GROUPS, (CHUNK + 1) * SSM_GROUP, STATE_F)
    bb = jnp.concatenate([bbr, bbi], axis=1)
    rp_r, rp_i = pw_r[CHUNK - 1::-1][:CHUNK], pw_i[CHUNK - 1::-1][:CHUNK]
    pcols = jnp.concatenate([jnp.concatenate([rp_r, rp_r], axis=-1).transpose(1, 2, 0),
                             jnp.concatenate([-rp_i, rp_i], axis=-1).transpose(1, 2, 0)], axis=-1)
    mzt = _s5_prep(ca, bb, pcols)
    wt = ca[:, SSM_GROUP:, :]

    sr, si = pw_r[CHUNK], pw_i[CHUNK]
    c1, c2 = [], []
    for _ in range(n_steps):
        c1.append(jnp.concatenate([sr, sr], axis=-1)[None])
        c2.append(jnp.concatenate([-si, si], axis=-1)[None])
        sr, si = sr * sr - si * si, 2.0 * sr * si
    pad = jnp.zeros((8 - n_steps, N_SSM_GROUPS, STATE_F), f32)
    coef = jnp.concatenate(c1 + [pad] + c2 + [pad], axis=0).transpose(1, 2, 0)
    return mzt.astype(BF16), wt.astype(BF16), coef


def _s5_kernel(ut_ref, mzt_ref, wt_ref, coef_ref, dcol_ref, o_ref, *, nj, n_steps):
    ut = jnp.concatenate([ut_ref[t] for t in range(CHUNK)], axis=0)
    r = jnp.dot(mzt_ref[0], ut, preferred_element_type=F32)
    y0 = r[:CHUNK_F]
    t = r[CHUNK_F:]
    j = lax.broadcasted_iota(jnp.int32, t.shape, 1) & (nj - 1)
    for m in range(n_steps):
        d = 1 << m
        sh = jnp.where(j >= d, pltpu.roll(t, d, axis=1), 0.0)
        p1 = coef_ref[0, :, m:m + 1]
        p2 = coef_ref[0, :, 8 + m:9 + m]
        t = t + p1 * sh + p2 * pltpu.roll(sh, SSM_STATE, axis=0)
    s_prev = jnp.where(j >= 1, pltpu.roll(t, 1, axis=1), 0.0)
    y = y0 + jnp.dot(wt_ref[0], s_prev.astype(BF16), preferred_element_type=F32)
    ys = jax.nn.gelu(y + dcol_ref[0] * ut.astype(F32)).astype(BF16)
    for t_out in range(CHUNK):
        o_ref[t_out] = ys[t_out * SSM_GROUP:(t_out + 1) * SSM_GROUP]


def _s5(ut4, mzt, wt, coef, dcol, nj, n_steps):
    _, g, _, n_rows = ut4.shape
    blk = pl.BlockSpec((CHUNK, None, SSM_GROUP, n_rows), lambda i: (0, i, 0, 0))
    return pl.pallas_call(
        functools.partial(_s5_kernel, nj=nj, n_steps=n_steps),
        grid=(g,),
        in_specs=[
            blk,
            pl.BlockSpec((1, CHUNK_F + STATE_F, CHUNK_F), lambda i: (i, 0, 0)),
            pl.BlockSpec((1, CHUNK_F, STATE_F), lambda i: (i, 0, 0)),
            pl.BlockSpec((1, STATE_F, 16), lambda i: (i, 0, 0)),
            pl.BlockSpec((1, CHUNK_F, 1), lambda i: (i, 0, 0)),
        ],
        out_specs=blk,
        out_shape=jax.ShapeDtypeStruct(ut4.shape, BF16),
        compiler_params=pltpu.CompilerParams(
            dimension_semantics=("parallel",), vmem_limit_bytes=VMEM_LIMIT),
        name="s5",
    )(ut4, mzt, wt, coef, dcol)


def _conv_kernel(v_ref, vh_ref, cw_ref, cb_ref, g_ref, b_ref, c_ref, vext_ref, shift_ref, conv_ref, *, tl, sub):
    i = pl.program_id(1)
    n_ext = tl + HALO
    n_chunks = tl // CHUNK
    n_hist = HALO // CHUNK
    blocks = [slice(c * LANES, (c + 1) * LANES) for c in range(D_CONV // LANES)]
    for t in range(CHUNK):
        for c, lanes in enumerate(blocks):
            hist = vh_ref[t, HALO_CHUNKS - n_hist:HALO_CHUNKS, lanes]
            vext_ref[c, pl.ds(t, n_hist, stride=CHUNK), :] = jnp.where(i > 0, hist, 0.0)
            vext_ref[c, pl.ds(HALO + t, n_chunks, stride=CHUNK), :] = v_ref[t, :, lanes]
    for m in range(SUBLANES):
        shift_ref[m, :, 0:n_ext - m, :] = vext_ref[:, m:n_ext, :]
    off = HALO - (CONV_WIDTH - 1)
    for c, lanes in enumerate(blocks):
        for s in range(tl // sub):
            acc = jnp.zeros((sub, LANES), F32) + cb_ref[:, lanes]
            for k in range(CONV_WIDTH):
                q, m = divmod(off + k, SUBLANES)
                r0 = s * sub + q * SUBLANES
                acc = acc + cw_ref[k:k + 1, lanes] * shift_ref[m, c, r0:r0 + sub, :]
            conv_ref[c, s * sub:(s + 1) * sub, :] = acc
    xs = [conv_ref[c] for c in range(len(blocks))]
    mu = sum(jnp.sum(x, axis=-1, keepdims=True) for x in xs) * (1.0 / D_CONV)
    xc = [x - mu for x in xs]
    var = sum(jnp.sum(x * x, axis=-1, keepdims=True) for x in xc) * (1.0 / D_CONV)
    inv = lax.rsqrt(var + LN_EPS)
    for c, lanes in enumerate(blocks):
        conv_ref[c] = jax.nn.silu(xc[c] * inv * g_ref[:, lanes] + b_ref[:, lanes])
    for t in range(CHUNK):
        for c, lanes in enumerate(blocks):
            c_ref[t, :, lanes] = conv_ref[c, pl.ds(t, n_chunks, stride=CHUNK), :].astype(BF16)


def _conv(vt, b, l, conv_w, conv_b, ln_g, ln_b, tl):
    sub = min(64, tl)
    n_chunks = tl // CHUNK
    assert n_chunks % HALO_CHUNKS == 0
    tiles = l // tl
    const = lambda bi, i: (0, 0)
    row_spec = pl.BlockSpec((CHUNK, n_chunks, D_CONV), lambda bi, i: (0, bi * tiles + i, 0))
    halo_spec = pl.BlockSpec(
        (CHUNK, HALO_CHUNKS, D_CONV),
        lambda bi, i: (0, jnp.maximum((bi * tiles + i) * (n_chunks // HALO_CHUNKS) - 1, 0), 0))
    return pl.pallas_call(
        functools.partial(_conv_kernel, tl=tl, sub=sub),
        grid=(b, l // tl),
        in_specs=[row_spec, halo_spec,
                  pl.BlockSpec(conv_w.shape, const), pl.BlockSpec(conv_b.shape, const),
                  pl.BlockSpec(ln_g.shape, const), pl.BlockSpec(ln_b.shape, const)],
        out_specs=row_spec,
        out_shape=jax.ShapeDtypeStruct((CHUNK, b * l // CHUNK, D_CONV), BF16),
        scratch_shapes=[
            pltpu.VMEM((D_CONV // LANES, tl + HALO, LANES), F32),
            pltpu.VMEM((SUBLANES, D_CONV // LANES, tl + HALO, LANES), F32),
            pltpu.VMEM((D_CONV // LANES, tl, LANES), F32),
        ],
        compiler_params=pltpu.CompilerParams(
            dimension_semantics=("parallel", "arbitrary"), vmem_limit_bytes=VMEM_LIMIT),
        name="conv",
    )(vt, vt, conv_w, conv_b, ln_g, ln_b)


def _mix_kernel(x_ref, yst_ref, c_ref,
                wg_ref, bg_ref, wglu_ref, bglu_ref, wco_ref, bco_ref,
                wo_ref, bo_ref, l1g_ref, l1b_ref, wrh_ref, wrl_ref, br_ref,
                h_ref, route_ref, *, alpha):
    c_out = jnp.dot(c_ref[...], wco_ref[...], preferred_element_type=F32) + bco_ref[...]

    glu = lax.dot_general(yst_ref[...], wglu_ref[...], _TN, preferred_element_type=F32) + bglu_ref[...]
    s_out = glu[:, :D_MODEL] * jax.nn.sigmoid(glu[:, D_MODEL:])

    x = x_ref[...]
    gates = jnp.dot(x.astype(BF16), wg_ref[...], preferred_element_type=F32) + bg_ref[...]
    merged = jax.nn.sigmoid(gates[:, :D_MODEL]) * s_out + jax.nn.sigmoid(gates[:, D_MODEL:]) * c_out
    mix = jnp.dot(merged.astype(BF16), wo_ref[...], preferred_element_type=F32) + bo_ref[...]
    h = _layer_norm(alpha * x + mix, l1g_ref[...], l1b_ref[...])
    rg = h.shape[0]
    for s in range(TOKEN_SUBLANES):
        h_ref[pl.ds(s, rg, stride=TOKEN_SUBLANES), :] = h[:, s * LANES:(s + 1) * LANES]

    h_hi = h.astype(BF16)
    h_lo = (h - h_hi.astype(F32)).astype(BF16)
    z = (jnp.dot(h_hi, wrh_ref[...], preferred_element_type=F32)
         + jnp.dot(h_lo, wrh_ref[...], preferred_element_type=F32)
         + jnp.dot(h_hi, wrl_ref[...], preferred_element_type=F32)) + br_ref[...]
    lane = lax.broadcasted_iota(jnp.int32, z.shape, 1)
    lanef = lane.astype(F32)
    neg = jnp.float32(-jnp.inf)
    big = jnp.float32(1e9)
    gmask = lane < N_GROUPS
    zg = jnp.where(gmask, z, neg)
    gmax = jnp.max(zg, axis=1, keepdims=True)
    gsel = jnp.min(jnp.where(gmask & (zg == gmax), lanef, big), axis=1, keepdims=True)
    p_group = 1.0 / jnp.sum(jnp.where(gmask, jnp.exp(zg - gmax), 0.0), axis=1, keepdims=True)
    lo = N_GROUPS + N_EXP_PER_GROUP * gsel
    emask = (lanef >= lo) & (lanef < lo + N_EXP_PER_GROUP)
    ze = jnp.where(emask, z, neg)
    v1 = jnp.max(ze, axis=1, keepdims=True)
    i1 = jnp.min(jnp.where(emask & (ze == v1), lanef, big), axis=1, keepdims=True)
    emask2 = emask & (lanef != i1)
    ze2 = jnp.where(emask2, z, neg)
    v2 = jnp.max(ze2, axis=1, keepdims=True)
    i2 = jnp.min(jnp.where(emask2 & (ze2 == v2), lanef, big), axis=1, keepdims=True)
    e2 = jnp.exp(v2 - v1)
    inv = 1.0 / (1.0 + e2)
    g1 = p_group * inv
    g2 = p_group * (e2 * inv)
    a = jnp.minimum(i1, i2) - lo
    b = jnp.maximum(i1, i2) - lo
    cls = gsel * N_PAIRS + (a * (2 * N_EXP_PER_GROUP - 1 - a)) * 0.5 + (b - a - 1.0)
    out = jnp.where(lane == 0, i1 - N_GROUPS, 0.0)
    out = jnp.where(lane == 1, i2 - N_GROUPS, out)
    out = jnp.where(lane == 2, g1, out)
    out = jnp.where(lane == 3, g2, out)
    out = jnp.where(lane == 4, cls, out)
    route_ref[...] = out


def _mix(x2, yst, ct, wts, rg, alpha):
    n_rows = x2.shape[0]

    def tau_spec(f):
        return pl.BlockSpec((None, rg, f), lambda i, t: (t, i, 0))

    def full_spec(a):
        nd = a.ndim
        return pl.BlockSpec(a.shape, lambda i, t: (0,) * nd)

    return pl.pallas_call(
        functools.partial(_mix_kernel, alpha=alpha),
        grid=(n_rows // rg, CHUNK),
        in_specs=[pl.BlockSpec((rg, D_MODEL), lambda i, t: (i, t)),
                  pl.BlockSpec((None, D_SSM, rg), lambda i, t: (t, 0, i)),
                  tau_spec(D_CONV)]
                 + [full_spec(a) for a in wts],
        out_specs=[pl.BlockSpec((None, rg * TOKEN_SUBLANES, LANES), lambda i, t: (t, i, 0)),
                   tau_spec(ROUTE_LANES)],
        out_shape=[
            jax.ShapeDtypeStruct((CHUNK, n_rows * TOKEN_SUBLANES, LANES), F32),
            jax.ShapeDtypeStruct((CHUNK, n_rows, ROUTE_LANES), F32),
        ],
        compiler_params=pltpu.CompilerParams(
            dimension_semantics=("parallel", "arbitrary"), vmem_limit_bytes=VMEM_LIMIT),
        name="mix",
    )(x2, yst, ct, *wts)


def _gather_rows(idx_ref, base, src_hbm, dst, sem, n, rows_per):
    def issue(r, carry):
        src0 = pl.multiple_of(idx_ref[base + r] * rows_per, rows_per)
        dst0 = pl.multiple_of(r * rows_per, rows_per)
        pltpu.make_async_copy(src_hbm.at[pl.ds(src0, rows_per)], dst.at[pl.ds(dst0, rows_per)], sem).start()
        return carry
    lax.fori_loop(0, n, issue, 0, unroll=8)


def _gather_row_groups(idx_ref, base, src_hbm, dst, sem, n_groups, rows_per):
    def issue(g, carry):
        for k in range(ROW_GROUP):
            r = g * ROW_GROUP + k
            src0 = pl.multiple_of(idx_ref[base + r] * rows_per, rows_per)
            dst0 = pl.multiple_of(r * rows_per, rows_per)
            pltpu.make_async_copy(src_hbm.at[pl.ds(src0, rows_per)], dst.at[pl.ds(dst0, rows_per)], sem).start()
        return carry
    lax.fori_loop(0, n_groups, issue, 0)


def _wait_row_groups(src_hbm, dst, sem, n_groups, rows_per):
    rows = ROW_GROUP * rows_per

    def wait(g, carry):
        pltpu.make_async_copy(src_hbm.at[pl.ds(0, rows)], dst.at[pl.ds(0, rows)], sem).wait()
        return carry
    lax.fori_loop(0, n_groups, wait, 0)


def _wait_rows(src_hbm, dst, sem):
    pltpu.make_async_copy(src_hbm.at[pl.ds(0, dst.shape[0])], dst, sem).wait()


def _lane_blocks(ref2d, n, rows_per):
    return jnp.concatenate([ref2d[pl.ds(s, n, stride=rows_per), :] for s in range(rows_per)], axis=1)


def _expert(xb, wgu, wd):
    gu = jnp.dot(xb, wgu, preferred_element_type=F32)
    hid = jax.nn.silu(gu[:, :D_FF_EXPERT]) * gu[:, D_FF_EXPERT:]
    return jnp.dot(hid.astype(BF16), wd, preferred_element_type=F32)


def _moe_kernel(ea_ref, eb_ref, groups_ref, tok_ref, h_hbm, wgu_a_ref, wgu_b_ref, wd_a_ref, wd_b_ref,
                y_ref, xbuf, sem, *, tm):
    del ea_ref, eb_ref
    i = pl.program_id(0)
    n = pl.num_programs(0)
    slot = i % GATHER_SLOTS

    def gather(blk):
        n_groups = groups_ref[jnp.minimum(blk, n - 1)]

        @pl.when((blk < n) & (n_groups > 0))
        def _():
            s = blk % GATHER_SLOTS
            _gather_row_groups(tok_ref, blk * tm, h_hbm, xbuf.at[s], sem.at[s], n_groups, TOKEN_SUBLANES)

    @pl.when(i == 0)
    def _():
        xbuf[...] = jnp.zeros_like(xbuf)
        for blk in range(GATHER_SLOTS - 1):
            gather(jnp.int32(blk))

    @pl.when(groups_ref[i] > 0)
    def _():
        _wait_row_groups(h_hbm, xbuf.at[slot], sem.at[slot], groups_ref[i], TOKEN_SUBLANES)
        xb = _lane_blocks(xbuf.at[slot], tm, TOKEN_SUBLANES).astype(BF16)
        ya = _expert(xb, wgu_a_ref[0], wd_a_ref[0])
        yb = _expert(xb, wgu_b_ref[0], wd_b_ref[0])
        for k, y in enumerate((ya, yb)):
            for s in range(TOKEN_SUBLANES):
                y_ref[pl.ds(k * TOKEN_SUBLANES + s, tm, stride=PAIR_SUBLANES), :] = y[:, s * LANES:(s + 1) * LANES]

    @pl.when(groups_ref[i] == 0)
    def _():
        y_ref[...] = jnp.zeros_like(y_ref)

    gather(i + GATHER_SLOTS - 1)


def _moe(blk_ea, blk_eb, blk_groups, tok_rows, h2, w_gu, w_d, tm):
    n_blocks = blk_ea.shape[0]
    grid_spec = pltpu.PrefetchScalarGridSpec(
        num_scalar_prefetch=4,
        grid=(n_blocks,),
        in_specs=[
            pl.BlockSpec(memory_space=pl.ANY),
            pl.BlockSpec((1, D_MODEL, 2 * D_FF_EXPERT), lambda i, ea, eb, groups, tok: (ea[i], 0, 0)),
            pl.BlockSpec((1, D_MODEL, 2 * D_FF_EXPERT), lambda i, ea, eb, groups, tok: (eb[i], 0, 0)),
            pl.BlockSpec((1, D_FF_EXPERT, D_MODEL), lambda i, ea, eb, groups, tok: (ea[i], 0, 0)),
            pl.BlockSpec((1, D_FF_EXPERT, D_MODEL), lambda i, ea, eb, groups, tok: (eb[i], 0, 0)),
        ],
        out_specs=pl.BlockSpec((tm * PAIR_SUBLANES, LANES), lambda i, ea, eb, groups, tok: (i, 0)),
        scratch_shapes=[pltpu.VMEM((GATHER_SLOTS, tm * TOKEN_SUBLANES, LANES), F32),
                        pltpu.SemaphoreType.DMA((GATHER_SLOTS,))],
    )
    return pl.pallas_call(
        functools.partial(_moe_kernel, tm=tm),
        grid_spec=grid_spec,
        out_shape=jax.ShapeDtypeStruct((n_blocks * tm * PAIR_SUBLANES, LANES), F32),
        compiler_params=pltpu.CompilerParams(
            dimension_semantics=("arbitrary",), vmem_limit_bytes=VMEM_LIMIT),
        name="moe",
    )(blk_ea, blk_eb, blk_groups, tok_rows, h2, w_gu, w_gu, w_d, w_d)


def _combine_kernel(pos_ref, h_ref, route_ref, y_hbm, g_ref, b_ref, o_ref, buf, res_ref, sem, *, tt, alpha):
    i = pl.program_id(0)
    n = pl.num_programs(0)
    slot = i % 2
    n_r = tt // CHUNK

    @pl.when(i == 0)
    def _():
        _gather_rows(pos_ref, 0, y_hbm, buf.at[0], sem.at[0], tt, PAIR_SUBLANES)

    @pl.when(i + 1 < n)
    def _():
        _gather_rows(pos_ref, (i + 1) * tt, y_hbm, buf.at[1 - slot], sem.at[1 - slot], tt, PAIR_SUBLANES)

    _wait_rows(y_hbm, buf.at[slot], sem.at[slot])
    y2 = _lane_blocks(buf.at[slot], tt, PAIR_SUBLANES)
    h = jnp.concatenate([_lane_blocks(h_ref.at[t], n_r, TOKEN_SUBLANES) for t in range(CHUNK)], axis=0)
    route = route_ref[...].reshape(tt, ROUTE_LANES)
    first_is_a = route[:, 0:1] < route[:, 1:2]
    g1 = route[:, 2:3]
    g2 = route[:, 3:4]
    wa = jnp.where(first_is_a, g1, g2)
    wb = jnp.where(first_is_a, g2, g1)
    moe = wa * y2[:, :D_MODEL] + wb * y2[:, D_MODEL:]
    res = _layer_norm(alpha * h + moe, g_ref[...], b_ref[...])
    n_cb = D_MODEL // LANES
    for c in range(n_cb):
        res_ref[c] = res[:, c * LANES:(c + 1) * LANES]
    for r in range(n_r):
        for c in range(n_cb):
            o_ref[r * CHUNK:(r + 1) * CHUNK, c * LANES:(c + 1) * LANES] = res_ref[c, pl.ds(r, CHUNK, stride=n_r), :]


def _combine(pos_rows, ht, routet, y_buf, ln_g, ln_b, tt, alpha):
    n_tok = routet.shape[0] * routet.shape[1]
    n_tiles = n_tok // tt
    n_r = tt // CHUNK
    grid_spec = pltpu.PrefetchScalarGridSpec(
        num_scalar_prefetch=1,
        grid=(n_tiles,),
        in_specs=[
            pl.BlockSpec((CHUNK, n_r * TOKEN_SUBLANES, LANES), lambda i, pos: (0, i, 0)),
            pl.BlockSpec((CHUNK, n_r, ROUTE_LANES), lambda i, pos: (0, i, 0)),
            pl.BlockSpec(memory_space=pl.ANY),
            pl.BlockSpec((1, D_MODEL), lambda i, pos: (0, 0)),
            pl.BlockSpec((1, D_MODEL), lambda i, pos: (0, 0)),
        ],
        out_specs=pl.BlockSpec((tt, D_MODEL), lambda i, pos: (i, 0)),
        scratch_shapes=[
            pltpu.VMEM((2, tt * PAIR_SUBLANES, LANES), F32),
            pltpu.VMEM((D_MODEL // LANES, tt, LANES), F32),
            pltpu.SemaphoreType.DMA((2,)),
        ],
    )
    return pl.pallas_call(
        functools.partial(_combine_kernel, tt=tt, alpha=alpha),
        grid_spec=grid_spec,
        out_shape=jax.ShapeDtypeStruct((n_tok, D_MODEL), F32),
        compiler_params=pltpu.CompilerParams(
            dimension_semantics=("arbitrary",), vmem_limit_bytes=VMEM_LIMIT),
        name="combine",
    )(pos_rows, ht, routet, y_buf, ln_g, ln_b)


def _class_experts():
    ea, eb = [], []
    for g in range(N_GROUPS):
        for a in range(N_EXP_PER_GROUP):
            for b in range(a + 1, N_EXP_PER_GROUP):
                ea.append(g * N_EXP_PER_GROUP + a)
                eb.append(g * N_EXP_PER_GROUP + b)
    return jnp.asarray(ea, jnp.int32), jnp.asarray(eb, jnp.int32)


def _lookup(table, idx):
    n = table.shape[0]
    hit = idx[..., None] == jnp.arange(n, dtype=idx.dtype)
    return jnp.sum(jnp.where(hit, table, 0), axis=-1).astype(table.dtype)


def _dispatch(cls, tm):
    n_tok = cls.shape[0]
    i32 = jnp.int32
    order = jnp.argsort(cls).astype(i32)
    inv = jnp.argsort(order).astype(i32)
    counts = jnp.sum(cls[:, None] == jnp.arange(N_CLASSES, dtype=i32)[None, :], axis=0).astype(i32)
    start = jnp.cumsum(counts) - counts
    pcounts = (counts + tm - 1) // tm * tm
    pend = jnp.cumsum(pcounts)
    pstart = pend - pcounts
    n_blocks = -(-n_tok // tm) + N_CLASSES
    blk_start = jnp.arange(n_blocks, dtype=i32) * tm
    blk_cls = jnp.minimum(jnp.sum(blk_start[:, None] >= pend[None, :], axis=1), N_CLASSES - 1).astype(i32)
    cls_ea, cls_eb = _class_experts()
    blk_first = blk_start - _lookup(pstart, blk_cls)
    blk_count = _lookup(counts, blk_cls)
    blk_groups = ((jnp.clip(blk_count - blk_first, 0, tm) + ROW_GROUP - 1) // ROW_GROUP).astype(i32)
    within = blk_first[:, None] + jnp.arange(tm, dtype=i32)[None, :]
    valid = within < blk_count[:, None]
    src = jnp.clip(_lookup(start, blk_cls)[:, None] + within, 0, n_tok - 1)
    tok_buf = jnp.where(valid, order[src.reshape(-1)].reshape(n_blocks, tm), 0).astype(i32)
    pos = (_lookup(pstart - start, cls) + inv).astype(i32)
    return _lookup(cls_ea, blk_cls), _lookup(cls_eb, blk_cls), blk_groups, tok_buf.reshape(-1), pos


def _layer(x, p, alpha):
    (w_in, b_in, lam_re, lam_im, log_dt, ssm_b_re, ssm_b_im, ssm_c_re, ssm_c_im,
     ssm_d, w_glu, b_glu, conv_w, conv_b, ln_c_g, ln_c_b, w_cout, b_cout, w_out, b_out,
     ln1_g, ln1_b, w_rg, b_rg, w_re, b_re, w_gate, w_up, w_down, ln2_g, ln2_b) = p
    bsz, seq, _ = x.shape
    n_tok = bsz * seq
    assert seq % CHUNK == 0
    nj = seq // CHUNK
    n_rows = bsz * nj
    assert nj & (nj - 1) == 0, "chunks per sequence must be a power of two"
    n_steps = max(1, int(math.log2(nj)))
    assert n_steps <= 8

    n_ucv = D_SSM + 2 * D_CONV
    row = lambda a: a.reshape(1, -1).astype(F32)
    rg = min(512, n_rows)

    x2 = x.reshape(n_rows, CHUNK * D_MODEL)
    ut, vt = _inproj(x2, w_in[:, :D_SSM].T.astype(BF16), b_in[:D_SSM].reshape(-1, 1).astype(F32),
                     w_in[:, D_SSM:n_ucv].astype(BF16), row(b_in[D_SSM:n_ucv]), rg)

    mzt, wt_state, coef = _s5_chunk_weights(lam_re, lam_im, log_dt, ssm_b_re, ssm_b_im,
                                            ssm_c_re, ssm_c_im, n_steps)
    dcol = jnp.tile(ssm_d.astype(F32).reshape(N_SSM_GROUPS, 1, SSM_GROUP), (1, CHUNK, 1))
    dcol = dcol.reshape(N_SSM_GROUPS, CHUNK_F, 1)
    yst = _s5(ut.reshape(CHUNK, N_SSM_GROUPS, SSM_GROUP, n_rows), mzt, wt_state, coef, dcol, nj, n_steps)
    yst = yst.reshape(CHUNK, D_SSM, n_rows)

    tl = min(512, seq)
    c_act = _conv(vt, bsz, seq, conv_w.reshape(CONV_WIDTH, D_CONV).astype(F32),
                  row(conv_b), row(ln_c_g), row(ln_c_b), tl)

    w_r = jnp.zeros((D_MODEL, ROUTE_LANES), F32)
    w_r = w_r.at[:, :N_GROUPS].set(w_rg).at[:, N_GROUPS:N_GROUPS + N_EXPERTS].set(w_re)
    b_r = jnp.zeros((1, ROUTE_LANES), F32)
    b_r = b_r.at[0, :N_GROUPS].set(b_rg).at[0, N_GROUPS:N_GROUPS + N_EXPERTS].set(b_re)
    w_r_hi = w_r.astype(BF16)
    w_r_lo = (w_r - w_r_hi.astype(F32)).astype(BF16)
    wts = (w_in[:, n_ucv:].astype(BF16), row(b_in[n_ucv:]),
           w_glu.astype(BF16), row(b_glu),
           w_cout.astype(BF16), row(b_cout), w_out.astype(BF16), row(b_out),
           row(ln1_g), row(ln1_b), w_r_hi, w_r_lo, b_r)
    ht, routet = _mix(x2, yst, c_act, wts, rg, alpha)

    tm_moe = 128
    cls = routet[:, :, 4].reshape(n_tok).astype(jnp.int32)
    blk_ea, blk_eb, blk_groups, tok_rows, pos = _dispatch(cls, tm_moe)
    w_gu = jnp.concatenate([w_gate, w_up], axis=-1).astype(BF16)
    y_buf = _moe(blk_ea, blk_eb, blk_groups, tok_rows, ht.reshape(n_tok * TOKEN_SUBLANES, LANES), w_gu,
                 w_down.astype(BF16), tm_moe)
    tt = CHUNK * CHUNK
    n_r = tt // CHUNK
    pos_rows = pos.reshape(CHUNK, n_rows // n_r, n_r).transpose(1, 0, 2).reshape(n_tok)
    out = _combine(pos_rows, ht, routet, y_buf, row(ln2_g), row(ln2_b), tt, alpha)
    return out.reshape(bsz, seq, D_MODEL)


def kernel(x, w_in, b_in, lam_re, lam_im, log_dt, ssm_b_re, ssm_b_im, ssm_c_re, ssm_c_im, ssm_d, w_glu, b_glu, conv_w, conv_b, ln_c_g, ln_c_b, w_cout, b_cout, w_out, b_out, ln1_g, ln1_b, w_route_group, b_route_group, w_route_expert, b_route_expert, w_gate, w_up, w_down, ln2_g, ln2_b):
    params = (w_in, b_in, lam_re, lam_im, log_dt, ssm_b_re, ssm_b_im, ssm_c_re, ssm_c_im,
              ssm_d, w_glu, b_glu, conv_w, conv_b, ln_c_g, ln_c_b, w_cout, b_cout, w_out, b_out,
              ln1_g, ln1_b, w_route_group, b_route_group, w_route_expert, b_route_expert,
              w_gate, w_up, w_down, ln2_g, ln2_b)
    depth = w_in.shape[0]
    alpha = (2.0 * depth) ** 0.25
    h = x
    for i in range(depth):
        h = _layer(h, tuple(p[i] for p in params), alpha)
    return h
```

```python
import functools
import math

import jax
import jax.numpy as jnp
from jax import lax
from jax.experimental import pallas as pl
from jax.experimental.pallas import tpu as pltpu

F32 = jnp.float32
BF16 = jnp.bfloat16
HIGHEST = lax.Precision.HIGHEST

D_MODEL = 1024
D_SSM = 512
SSM_GROUP = 16
N_SSM_GROUPS = D_SSM // SSM_GROUP
SSM_STATE = 64
D_CONV = 512
CONV_WIDTH = 31
N_GROUPS = 4
N_EXP_PER_GROUP = 8
N_EXPERTS = N_GROUPS * N_EXP_PER_GROUP
TOP_K_INNER = 2
N_PAIRS = N_EXP_PER_GROUP * (N_EXP_PER_GROUP - 1) // 2
N_CLASSES = N_GROUPS * N_PAIRS
D_FF_EXPERT = D_MODEL // 4
LN_EPS = 1e-5

CHUNK = 16
CHUNK_F = CHUNK * SSM_GROUP
STATE_F = 2 * SSM_STATE
ROUTE_LANES = 128
HALO = 32
HALO_CHUNKS = 8
SUBLANES = 8
LANES = 128
TOKEN_SUBLANES = D_MODEL // LANES
PAIR_SUBLANES = TOP_K_INNER * TOKEN_SUBLANES
GATHER_SLOTS = 3
ROW_GROUP = 8
VMEM_LIMIT = 56 * 1024 * 1024

_NT = (((1,), (1,)), ((), ()))
_TN = (((0,), (0,)), ((), ()))


def _layer_norm(x, g, b):
    mu = jnp.mean(x, axis=-1, keepdims=True)
    xc = x - mu
    var = jnp.mean(xc * xc, axis=-1, keepdims=True)
    return xc * lax.rsqrt(var + LN_EPS) * g + b


def _inproj_kernel(*refs, nj):
    n_cb = D_MODEL // LANES
    x_refs = refs[:n_cb]
    wut_ref, but_ref, wcv_ref, bcv_ref, ut_ref, v_ref, xt_ref = refs[n_cb:]
    taus_per_dot = 4
    for q in range(CHUNK // taus_per_dot):
        taus = range(q * taus_per_dot, (q + 1) * taus_per_dot)
        x = jnp.concatenate(
            [jnp.concatenate([xr[pl.ds(t, nj, stride=CHUNK), :] for xr in x_refs], axis=1) for t in taus],
            axis=0).astype(BF16)
        ut = lax.dot_general(wut_ref[...], x, _NT, preferred_element_type=F32) + but_ref[...]
        p = jnp.dot(x, wcv_ref[...], preferred_element_type=F32) + bcv_ref[...]
        v = p[:, :D_CONV] * jax.nn.sigmoid(p[:, D_CONV:])
        for k, t in enumerate(taus):
            rows = slice(k * nj, (k + 1) * nj)
            ut_ref[t] = ut[:, rows].astype(BF16)
            v_ref[t] = v[rows]
            xt_ref[t] = x[rows]


def _inproj(x2d, bsz, seq, w_ut, b_ut, w_cv, b_cv):
    nj = seq // CHUNK
    n_rows = bsz * nj
    n_cb = D_MODEL // LANES
    const = lambda b: (0, 0)
    col_block = lambda c: pl.BlockSpec((seq, LANES), lambda b: (b, c))
    return pl.pallas_call(
        functools.partial(_inproj_kernel, nj=nj),
        grid=(bsz,),
        in_specs=[col_block(c) for c in range(n_cb)] + [
            pl.BlockSpec(w_ut.shape, const),
            pl.BlockSpec(b_ut.shape, const),
            pl.BlockSpec(w_cv.shape, const),
            pl.BlockSpec(b_cv.shape, const),
        ],
        out_specs=[
            pl.BlockSpec((CHUNK, D_SSM, nj), lambda b: (0, 0, b)),
            pl.BlockSpec((CHUNK, nj, D_CONV), lambda b: (0, b, 0)),
            pl.BlockSpec((CHUNK, nj, D_MODEL), lambda b: (0, b, 0)),
        ],
        out_shape=[
            jax.ShapeDtypeStruct((CHUNK, D_SSM, n_rows), BF16),
            jax.ShapeDtypeStruct((CHUNK, n_rows, D_CONV), F32),
            jax.ShapeDtypeStruct((CHUNK, n_rows, D_MODEL), BF16),
        ],
        compiler_params=pltpu.CompilerParams(
            dimension_semantics=("parallel",), vmem_limit_bytes=VMEM_LIMIT),
        name="inproj",
    )(*([x2d] * n_cb), w_ut, b_ut, w_cv, b_cv)


def _s5_prep_kernel(ca_ref, bb_ref, pc_ref, mzt_ref):
    bb = bb_ref[0]
    k_all = jnp.dot(ca_ref[0, :CHUNK_F, :], bb, precision=HIGHEST, preferred_element_type=F32)
    mzt_ref[0, :CHUNK_F, :] = jnp.zeros((CHUNK_F, CHUNK_F), F32)
    bb_sw = pltpu.roll(bb, SSM_STATE, axis=0)
    for t_in in range(CHUNK):
        cols = slice(t_in * SSM_GROUP, (t_in + 1) * SSM_GROUP)
        mzt_ref[0, t_in * SSM_GROUP:CHUNK_F, cols] = k_all[:(CHUNK - t_in) * SSM_GROUP, :]
        mzt_ref[0, CHUNK_F:, cols] = (pc_ref[0, :, t_in:t_in + 1] * bb
                                      + pc_ref[0, :, CHUNK + t_in:CHUNK + t_in + 1] * bb_sw)


def _s5_prep(ca, bb, pcols):
    g = ca.shape[0]
    return pl.pallas_call(
        _s5_prep_kernel,
        grid=(g,),
        in_specs=[pl.BlockSpec((1,) + ca.shape[1:], lambda i: (i, 0, 0)),
                  pl.BlockSpec((1,) + bb.shape[1:], lambda i: (i, 0, 0)),
                  pl.BlockSpec((1,) + pcols.shape[1:], lambda i: (i, 0, 0))],
        out_specs=pl.BlockSpec((1, CHUNK_F + STATE_F, CHUNK_F), lambda i: (i, 0, 0)),
        out_shape=jax.ShapeDtypeStruct((g, CHUNK_F + STATE_F, CHUNK_F), F32),
        compiler_params=pltpu.CompilerParams(dimension_semantics=("parallel",)),
        name="s5prep",
    )(ca, bb, pcols)


def _s5_chunk_weights(lam_re, lam_im, log_dt, b_re, b_im, c_re, c_im, n_steps):
    f32 = F32
    lr, li = lam_re.astype(f32), lam_im.astype(f32)
    dt = jnp.exp(log_dt.astype(f32))[:, None]
    mag = jnp.exp(lr * dt)
    ar, ai = mag * jnp.cos(li * dt), mag * jnp.sin(li * dt)
    den = lr * lr + li * li
    fr = ((ar - 1.0) * lr + ai * li) / den
    fi = (ai * lr - (ar - 1.0) * li) / den
    br, bi = b_re.astype(f32), b_im.astype(f32)
    bbr = fr[..., None] * br - fi[..., None] * bi
    bbi = fr[..., None] * bi + fi[..., None] * br
    cr, ci = c_re.astype(f32), c_im.astype(f32)

    kk_pow = jnp.arange(CHUNK + 1, dtype=f32)[:, None, None]
    mag_k = jnp.exp(kk_pow * (lr * dt)[None])
    pw_r = mag_k * jnp.cos(kk_pow * (li * dt)[None])
    pw_i = mag_k * jnp.sin(kk_pow * (li * dt)[None])

    ca_r = cr[None] * pw_r[:, :, None, :] - ci[None] * pw_i[:, :, None, :]
    ca_i = cr[None] * pw_i[:, :, None, :] + ci[None] * pw_r[:, :, None, :]
    ca = jnp.concatenate([ca_r, -ca_i], axis=-1).transpose(1, 0, 2, 3)
    ca = ca.reshape(N_SSM_GROUPS, (CHUNK + 1) * SSM_GROUP, STATE_F)
    bb = jnp.concatenate([bbr, bbi], axis=1)
    rp_r, rp_i = pw_r[CHUNK - 1::-1][:CHUNK], pw_i[CHUNK - 1::-1][:CHUNK]
    pcols = jnp.concatenate([jnp.concatenate([rp_r, rp_r], axis=-1).transpose(1, 2, 0),
                             jnp.concatenate([-rp_i, rp_i], axis=-1).transpose(1, 2, 0)], axis=-1)
    mzt = _s5_prep(ca, bb, pcols)
    wt = ca[:, SSM_GROUP:, :]

    sr, si = pw_r[CHUNK], pw_i[CHUNK]
    c1, c2 = [], []
    for _ in range(n_steps):
        c1.append(jnp.concatenate([sr, sr], axis=-1)[None])
        c2.append(jnp.concatenate([-si, si], axis=-1)[None])
        sr, si = sr * sr - si * si, 2.0 * sr * si
    pad = jnp.zeros((8 - n_steps, N_SSM_GROUPS, STATE_F), f32)
    coef = jnp.concatenate(c1 + [pad] + c2 + [pad], axis=0).transpose(1, 2, 0)
    return mzt.astype(BF16), wt.astype(BF16), coef


def _s5_kernel(ut_ref, mzt_ref, wt_ref, coef_ref, dcol_ref, o_ref, *, nj, n_steps):
    ut = jnp.concatenate([ut_ref[t] for t in range(CHUNK)], axis=0)
    r = jnp.dot(mzt_ref[0], ut, preferred_element_type=F32)
    y0 = r[:CHUNK_F]
    t = r[CHUNK_F:]
    j = lax.broadcasted_iota(jnp.int32, t.shape, 1) & (nj - 1)
    for m in range(n_steps):
        d = 1 << m
        sh = jnp.where(j >= d, pltpu.roll(t, d, axis=1), 0.0)
        p1 = coef_ref[0, :, m:m + 1]
        p2 = coef_ref[0, :, 8 + m:9 + m]
        t = t + p1 * sh + p2 * pltpu.roll(sh, SSM_STATE, axis=0)
    s_prev = jnp.where(j >= 1, pltpu.roll(t, 1, axis=1), 0.0)
    y = y0 + jnp.dot(wt_ref[0], s_prev.astype(BF16), preferred_element_type=F32)
    ys = jax.nn.gelu(y + dcol_ref[0] * ut.astype(F32)).astype(BF16)
    for t_out in range(CHUNK):
        o_ref[t_out] = ys[t_out * SSM_GROUP:(t_out + 1) * SSM_GROUP]


def _s5(ut4, mzt, wt, coef, dcol, nj, n_steps):
    _, g, _, n_rows = ut4.shape
    blk = pl.BlockSpec((CHUNK, None, SSM_GROUP, n_rows), lambda i: (0, i, 0, 0))
    return pl.pallas_call(
        functools.partial(_s5_kernel, nj=nj, n_steps=n_steps),
        grid=(g,),
        in_specs=[
            blk,
            pl.BlockSpec((1, CHUNK_F + STATE_F, CHUNK_F), lambda i: (i, 0, 0)),
            pl.BlockSpec((1, CHUNK_F, STATE_F), lambda i: (i, 0, 0)),
            pl.BlockSpec((1, STATE_F, 16), lambda i: (i, 0, 0)),
            pl.BlockSpec((1, CHUNK_F, 1), lambda i: (i, 0, 0)),
        ],
        out_specs=blk,
        out_shape=jax.ShapeDtypeStruct(ut4.shape, BF16),
        compiler_params=pltpu.CompilerParams(
            dimension_semantics=("parallel",), vmem_limit_bytes=VMEM_LIMIT),
        name="s5",
    )(ut4, mzt, wt, coef, dcol)


def _conv_kernel(v_ref, vh_ref, cw_ref, cb_ref, g_ref, b_ref, c_ref, vext_ref, shift_ref, conv_ref, *, tl, sub):
    i = pl.program_id(1)
    n_ext = tl + HALO
    n_chunks = tl // CHUNK
    n_hist = HALO // CHUNK
    blocks = [slice(c * LANES, (c + 1) * LANES) for c in range(D_CONV // LANES)]
    for t in range(CHUNK):
        for c, lanes in enumerate(blocks):
            hist = vh_ref[t, HALO_CHUNKS - n_hist:HALO_CHUNKS, lanes]
            vext_ref[c, pl.ds(t, n_hist, stride=CHUNK), :] = jnp.where(i > 0, hist, 0.0)
            vext_ref[c, pl.ds(HALO + t, n_chunks, stride=CHUNK), :] = v_ref[t, :, lanes]
    for m in range(SUBLANES):
        shift_ref[m, :, 0:n_ext - m, :] = vext_ref[:, m:n_ext, :]
    off = HALO - (CONV_WIDTH - 1)
    for c, lanes in enumerate(blocks):
        for s in range(tl // sub):
            acc = jnp.zeros((sub, LANES), F32) + cb_ref[:, lanes]
            for k in range(CONV_WIDTH):
                q, m = divmod(off + k, SUBLANES)
                r0 = s * sub + q * SUBLANES
                acc = acc + cw_ref[k:k + 1, lanes] * shift_ref[m, c, r0:r0 + sub, :]
            conv_ref[c, s * sub:(s + 1) * sub, :] = acc
    xs = [conv_ref[c] for c in range(len(blocks))]
    mu = sum(jnp.sum(x, axis=-1, keepdims=True) for x in xs) * (1.0 / D_CONV)
    xc = [x - mu for x in xs]
    var = sum(jnp.sum(x * x, axis=-1, keepdims=True) for x in xc) * (1.0 / D_CONV)
    inv = lax.rsqrt(var + LN_EPS)
    for c, lanes in enumerate(blocks):
        conv_ref[c] = jax.nn.silu(xc[c] * inv * g_ref[:, lanes] + b_ref[:, lanes])
    for t in range(CHUNK):
        for c, lanes in enumerate(blocks):
            c_ref[t, :, lanes] = conv_ref[c, pl.ds(t, n_chunks, stride=CHUNK), :].astype(BF16)


def _conv(vt, b, l, conv_w, conv_b, ln_g, ln_b, tl):
    sub = min(64, tl)
    n_chunks = tl // CHUNK
    assert n_chunks % HALO_CHUNKS == 0
    tiles = l // tl
    const = lambda bi, i: (0, 0)
    row_spec = pl.BlockSpec((CHUNK, n_chunks, D_CONV), lambda bi, i: (0, bi * tiles + i, 0))
    halo_spec = pl.BlockSpec(
        (CHUNK, HALO_CHUNKS, D_CONV),
        lambda bi, i: (0, jnp.maximum((bi * tiles + i) * (n_chunks // HALO_CHUNKS) - 1, 0), 0))
    return pl.pallas_call(
        functools.partial(_conv_kernel, tl=tl, sub=sub),
        grid=(b, l // tl),
        in_specs=[row_spec, halo_spec,
                  pl.BlockSpec(conv_w.shape, const), pl.BlockSpec(conv_b.shape, const),
                  pl.BlockSpec(ln_g.shape, const), pl.BlockSpec(ln_b.shape, const)],
        out_specs=row_spec,
        out_shape=jax.ShapeDtypeStruct((CHUNK, b * l // CHUNK, D_CONV), BF16),
        scratch_shapes=[
            pltpu.VMEM((D_CONV // LANES, tl + HALO, LANES), F32),
            pltpu.VMEM((SUBLANES, D_CONV // LANES, tl + HALO, LANES), F32),
            pltpu.VMEM((D_CONV // LANES, tl, LANES), F32),
        ],
        compiler_params=pltpu.CompilerParams(
            dimension_semantics=("parallel", "arbitrary"), vmem_limit_bytes=VMEM_LIMIT),
        name="conv",
    )(vt, vt, conv_w, conv_b, ln_g, ln_b)


def _mix_kernel(x_ref, yst_ref, c_ref,
                wg_ref, bg_ref, wglu_ref, bglu_ref, wco_ref, bco_ref,
                wo_ref, bo_ref, l1g_ref, l1b_ref, wrh_ref, wrl_ref, br_ref,
                h_ref, route_ref, *, alpha):
    c_out = jnp.dot(c_ref[...], wco_ref[...], preferred_element_type=F32) + bco_ref[...]

    glu = lax.dot_general(yst_ref[...], wglu_ref[...], _TN, preferred_element_type=F32) + bglu_ref[...]
    s_out = glu[:, :D_MODEL] * jax.nn.sigmoid(glu[:, D_MODEL:])

    xb = x_ref[...]
    gates = jnp.dot(xb, wg_ref[...], preferred_element_type=F32) + bg_ref[...]
    merged = jax.nn.sigmoid(gates[:, :D_MODEL]) * s_out + jax.nn.sigmoid(gates[:, D_MODEL:]) * c_out
    mix = jnp.dot(merged.astype(BF16), wo_ref[...], preferred_element_type=F32) + bo_ref[...]
    h = _layer_norm(alpha * xb.astype(F32) + mix, l1g_ref[...], l1b_ref[...])
    rg = h.shape[0]
    for s in range(TOKEN_SUBLANES):
        h_ref[pl.ds(s, rg, stride=TOKEN_SUBLANES), :] = h[:, s * LANES:(s + 1) * LANES]

    h_hi = h.astype(BF16)
    h_lo = (h - h_hi.astype(F32)).astype(BF16)
    z = (jnp.dot(h_hi, wrh_ref[...], preferred_element_type=F32)
         + jnp.dot(h_lo, wrh_ref[...], preferred_element_type=F32)
         + jnp.dot(h_hi, wrl_ref[...], preferred_element_type=F32)) + br_ref[...]
    lane = lax.broadcasted_iota(jnp.int32, z.shape, 1)
    lanef = lane.astype(F32)
    neg = jnp.float32(-jnp.inf)
    big = jnp.float32(1e9)
    gmask = lane < N_GROUPS
    zg = jnp.where(gmask, z, neg)
    gmax = jnp.max(zg, axis=1, keepdims=True)
    gsel = jnp.min(jnp.where(gmask & (zg == gmax), lanef, big), axis=1, keepdims=True)
    p_group = 1.0 / jnp.sum(jnp.where(gmask, jnp.exp(zg - gmax), 0.0), axis=1, keepdims=True)
    lo = N_GROUPS + N_EXP_PER_GROUP * gsel
    emask = (lanef >= lo) & (lanef < lo + N_EXP_PER_GROUP)
    ze = jnp.where(emask, z, neg)
    v1 = jnp.max(ze, axis=1, keepdims=True)
    i1 = jnp.min(jnp.where(emask & (ze == v1), lanef, big), axis=1, keepdims=True)
    emask2 = emask & (lanef != i1)
    ze2 = jnp.where(emask2, z, neg)
    v2 = jnp.max(ze2, axis=1, keepdims=True)
    i2 = jnp.min(jnp.where(emask2 & (ze2 == v2), lanef, big), axis=1, keepdims=True)
    e2 = jnp.exp(v2 - v1)
    inv = 1.0 / (1.0 + e2)
    g1 = p_group * inv
    g2 = p_group * (e2 * inv)
    a = jnp.minimum(i1, i2) - lo
    b = jnp.maximum(i1, i2) - lo
    cls = gsel * N_PAIRS + (a * (2 * N_EXP_PER_GROUP - 1 - a)) * 0.5 + (b - a - 1.0)
    out = jnp.where(lane == 0, i1 - N_GROUPS, 0.0)
    out = jnp.where(lane == 1, i2 - N_GROUPS, out)
    out = jnp.where(lane == 2, g1, out)
    out = jnp.where(lane == 3, g2, out)
    out = jnp.where(lane == 4, cls, out)
    route_ref[...] = out


def _mix(xt, yst, ct, wts, rg, alpha):
    n_rows = xt.shape[1]

    def tau_spec(f):
        return pl.BlockSpec((None, rg, f), lambda i, t: (t, i, 0))

    def full_spec(a):
        nd = a.ndim
        return pl.BlockSpec(a.shape, lambda i, t: (0,) * nd)

    return pl.pallas_call(
        functools.partial(_mix_kernel, alpha=alpha),
        grid=(n_rows // rg, CHUNK),
        in_specs=[tau_spec(D_MODEL),
                  pl.BlockSpec((None, D_SSM, rg), lambda i, t: (t, 0, i)),
                  tau_spec(D_CONV)]
                 + [full_spec(a) for a in wts],
        out_specs=[pl.BlockSpec((None, rg * TOKEN_SUBLANES, LANES), lambda i, t: (t, i, 0)),
                   tau_spec(ROUTE_LANES)],
        out_shape=[
            jax.ShapeDtypeStruct((CHUNK, n_rows * TOKEN_SUBLANES, LANES), F32),
            jax.ShapeDtypeStruct((CHUNK, n_rows, ROUTE_LANES), F32),
        ],
        compiler_params=pltpu.CompilerParams(
            dimension_semantics=("parallel", "arbitrary"), vmem_limit_bytes=VMEM_LIMIT),
        name="mix",
    )(xt, yst, ct, *wts)


def _gather_rows(idx_ref, base, src_hbm, dst, sem, n, rows_per):
    def issue(r, carry):
        src0 = pl.multiple_of(idx_ref[base + r] * rows_per, rows_per)
        dst0 = pl.multiple_of(r * rows_per, rows_per)
        pltpu.make_async_copy(src_hbm.at[pl.ds(src0, rows_per)], dst.at[pl.ds(dst0, rows_per)], sem).start()
        return carry
    lax.fori_loop(0, n, issue, 0, unroll=8)


def _gather_row_groups(idx_ref, base, src_hbm, dst, sem, n_groups, rows_per):
    def issue(g, carry):
        for k in range(ROW_GROUP):
            r = g * ROW_GROUP + k
            src0 = pl.multiple_of(idx_ref[base + r] * rows_per, rows_per)
            dst0 = pl.multiple_of(r * rows_per, rows_per)
            pltpu.make_async_copy(src_hbm.at[pl.ds(src0, rows_per)], dst.at[pl.ds(dst0, rows_per)], sem).start()
        return carry
    lax.fori_loop(0, n_groups, issue, 0)


def _wait_row_groups(src_hbm, dst, sem, n_groups, rows_per):
    rows = ROW_GROUP * rows_per

    def wait(g, carry):
        pltpu.make_async_copy(src_hbm.at[pl.ds(0, rows)], dst.at[pl.ds(0, rows)], sem).wait()
        return carry
    lax.fori_loop(0, n_groups, wait, 0)


def _wait_rows(src_hbm, dst, sem):
    pltpu.make_async_copy(src_hbm.at[pl.ds(0, dst.shape[0])], dst, sem).wait()


def _lane_blocks(ref2d, n, rows_per):
    return jnp.concatenate([ref2d[pl.ds(s, n, stride=rows_per), :] for s in range(rows_per)], axis=1)


def _expert(xb, wgu, wd):
    gu = jnp.dot(xb, wgu, preferred_element_type=F32)
    hid = jax.nn.silu(gu[:, :D_FF_EXPERT]) * gu[:, D_FF_EXPERT:]
    return jnp.dot(hid.astype(BF16), wd, preferred_element_type=F32)


def _moe_kernel(ea_ref, eb_ref, groups_ref, tok_ref, h_hbm, wgu_a_ref, wgu_b_ref, wd_a_ref, wd_b_ref,
                y_ref, xbuf, sem, *, tm):
    del ea_ref, eb_ref
    i = pl.program_id(0)
    n = pl.num_programs(0)
    slot = i % GATHER_SLOTS

    def gather(blk):
        n_groups = groups_ref[jnp.minimum(blk, n - 1)]

        @pl.when((blk < n) & (n_groups > 0))
        def _():
            s = blk % GATHER_SLOTS
            _gather_row_groups(tok_ref, blk * tm, h_hbm, xbuf.at[s], sem.at[s], n_groups, TOKEN_SUBLANES)

    @pl.when(i == 0)
    def _():
        xbuf[...] = jnp.zeros_like(xbuf)
        for blk in range(GATHER_SLOTS - 1):
            gather(jnp.int32(blk))

    @pl.when(groups_ref[i] > 0)
    def _():
        _wait_row_groups(h_hbm, xbuf.at[slot], sem.at[slot], groups_ref[i], TOKEN_SUBLANES)
        xb = _lane_blocks(xbuf.at[slot], tm, TOKEN_SUBLANES).astype(BF16)
        ya = _expert(xb, wgu_a_ref[0], wd_a_ref[0])
        yb = _expert(xb, wgu_b_ref[0], wd_b_ref[0])
        for k, y in enumerate((ya, yb)):
            for s in range(TOKEN_SUBLANES):
                y_ref[pl.ds(k * TOKEN_SUBLANES + s, tm, stride=PAIR_SUBLANES), :] = y[:, s * LANES:(s + 1) * LANES]

    @pl.when(groups_ref[i] == 0)
    def _():
        y_ref[...] = jnp.zeros_like(y_ref)

    gather(i + GATHER_SLOTS - 1)


def _moe(blk_ea, blk_eb, blk_groups, tok_rows, h2, w_gu, w_d, tm):
    n_blocks = blk_ea.shape[0]
    grid_spec = pltpu.PrefetchScalarGridSpec(
        num_scalar_prefetch=4,
        grid=(n_blocks,),
        in_specs=[
            pl.BlockSpec(memory_space=pl.ANY),
            pl.BlockSpec((1, D_MODEL, 2 * D_FF_EXPERT), lambda i, ea, eb, groups, tok: (ea[i], 0, 0)),
            pl.BlockSpec((1, D_MODEL, 2 * D_FF_EXPERT), lambda i, ea, eb, groups, tok: (eb[i], 0, 0)),
            pl.BlockSpec((1, D_FF_EXPERT, D_MODEL), lambda i, ea, eb, groups, tok: (ea[i], 0, 0)),
            pl.BlockSpec((1, D_FF_EXPERT, D_MODEL), lambda i, ea, eb, groups, tok: (eb[i], 0, 0)),
        ],
        out_specs=pl.BlockSpec((tm * PAIR_SUBLANES, LANES), lambda i, ea, eb, groups, tok: (i, 0)),
        scratch_shapes=[pltpu.VMEM((GATHER_SLOTS, tm * TOKEN_SUBLANES, LANES), F32),
                        pltpu.SemaphoreType.DMA((GATHER_SLOTS,))],
    )
    return pl.pallas_call(
        functools.partial(_moe_kernel, tm=tm),
        grid_spec=grid_spec,
        out_shape=jax.ShapeDtypeStruct((n_blocks * tm * PAIR_SUBLANES, LANES), F32),
        compiler_params=pltpu.CompilerParams(
            dimension_semantics=("arbitrary",), vmem_limit_bytes=VMEM_LIMIT),
        name="moe",
    )(blk_ea, blk_eb, blk_groups, tok_rows, h2, w_gu, w_gu, w_d, w_d)


def _combine_kernel(pos_ref, h_ref, route_ref, y_hbm, g_ref, b_ref, o_ref, buf, res_ref, sem, *, tt, alpha):
    i = pl.program_id(0)
    n = pl.num_programs(0)
    slot = i % 2
    n_r = tt // CHUNK

    @pl.when(i == 0)
    def _():
        _gather_rows(pos_ref, 0, y_hbm, buf.at[0], sem.at[0], tt, PAIR_SUBLANES)

    @pl.when(i + 1 < n)
    def _():
        _gather_rows(pos_ref, (i + 1) * tt, y_hbm, buf.at[1 - slot], sem.at[1 - slot], tt, PAIR_SUBLANES)

    _wait_rows(y_hbm, buf.at[slot], sem.at[slot])
    y2 = _lane_blocks(buf.at[slot], tt, PAIR_SUBLANES)
    h = jnp.concatenate([_lane_blocks(h_ref.at[t], n_r, TOKEN_SUBLANES) for t in range(CHUNK)], axis=0)
    route = route_ref[...].reshape(tt, ROUTE_LANES)
    first_is_a = route[:, 0:1] < route[:, 1:2]
    g1 = route[:, 2:3]
    g2 = route[:, 3:4]
    wa = jnp.where(first_is_a, g1, g2)
    wb = jnp.where(first_is_a, g2, g1)
    moe = wa * y2[:, :D_MODEL] + wb * y2[:, D_MODEL:]
    res = _layer_norm(alpha * h + moe, g_ref[...], b_ref[...])
    n_cb = D_MODEL // LANES
    for c in range(n_cb):
        res_ref[c] = res[:, c * LANES:(c + 1) * LANES]
    for r in range(n_r):
        for c in range(n_cb):
            o_ref[r * CHUNK:(r + 1) * CHUNK, c * LANES:(c + 1) * LANES] = res_ref[c, pl.ds(r, CHUNK, stride=n_r), :]


def _combine(pos_rows, ht, routet, y_buf, ln_g, ln_b, tt, alpha):
    n_tok = routet.shape[0] * routet.shape[1]
    n_tiles = n_tok // tt
    n_r = tt // CHUNK
    grid_spec = pltpu.PrefetchScalarGridSpec(
        num_scalar_prefetch=1,
        grid=(n_tiles,),
        in_specs=[
            pl.BlockSpec((CHUNK, n_r * TOKEN_SUBLANES, LANES), lambda i, pos: (0, i, 0)),
            pl.BlockSpec((CHUNK, n_r, ROUTE_LANES), lambda i, pos: (0, i, 0)),
            pl.BlockSpec(memory_space=pl.ANY),
            pl.BlockSpec((1, D_MODEL), lambda i, pos: (0, 0)),
            pl.BlockSpec((1, D_MODEL), lambda i, pos: (0, 0)),
        ],
        out_specs=pl.BlockSpec((tt, D_MODEL), lambda i, pos: (i, 0)),
        scratch_shapes=[
            pltpu.VMEM((2, tt * PAIR_SUBLANES, LANES), F32),
            pltpu.VMEM((D_MODEL // LANES, tt, LANES), F32),
            pltpu.SemaphoreType.DMA((2,)),
        ],
    )
    return pl.pallas_call(
        functools.partial(_combine_kernel, tt=tt, alpha=alpha),
        grid_spec=grid_spec,
        out_shape=jax.ShapeDtypeStruct((n_tok, D_MODEL), F32),
        compiler_params=pltpu.CompilerParams(
            dimension_semantics=("arbitrary",), vmem_limit_bytes=VMEM_LIMIT),
        name="combine",
    )(pos_rows, ht, routet, y_buf, ln_g, ln_b)


def _class_experts():
    ea, eb = [], []
    for g in range(N_GROUPS):
        for a in range(N_EXP_PER_GROUP):
            for b in range(a + 1, N_EXP_PER_GROUP):
                ea.append(g * N_EXP_PER_GROUP + a)
                eb.append(g * N_EXP_PER_GROUP + b)
    return jnp.asarray(ea, jnp.int32), jnp.asarray(eb, jnp.int32)


def _lookup(table, idx):
    n = table.shape[0]
    hit = idx[..., None] == jnp.arange(n, dtype=idx.dtype)
    return jnp.sum(jnp.where(hit, table, 0), axis=-1).astype(table.dtype)


def _dispatch(cls, tm):
    n_tok = cls.shape[0]
    i32 = jnp.int32
    order = jnp.argsort(cls).astype(i32)
    inv = jnp.argsort(order).astype(i32)
    counts = jnp.sum(cls[:, None] == jnp.arange(N_CLASSES, dtype=i32)[None, :], axis=0).astype(i32)
    start = jnp.cumsum(counts) - counts
    pcounts = (counts + tm - 1) // tm * tm
    pend = jnp.cumsum(pcounts)
    pstart = pend - pcounts
    n_blocks = -(-n_tok // tm) + N_CLASSES
    blk_start = jnp.arange(n_blocks, dtype=i32) * tm
    blk_cls = jnp.minimum(jnp.sum(blk_start[:, None] >= pend[None, :], axis=1), N_CLASSES - 1).astype(i32)
    cls_ea, cls_eb = _class_experts()
    blk_first = blk_start - _lookup(pstart, blk_cls)
    blk_count = _lookup(counts, blk_cls)
    blk_groups = ((jnp.clip(blk_count - blk_first, 0, tm) + ROW_GROUP - 1) // ROW_GROUP).astype(i32)
    within = blk_first[:, None] + jnp.arange(tm, dtype=i32)[None, :]
    valid = within < blk_count[:, None]
    src = jnp.clip(_lookup(start, blk_cls)[:, None] + within, 0, n_tok - 1)
    tok_buf = jnp.where(valid, order[src.reshape(-1)].reshape(n_blocks, tm), 0).astype(i32)
    pos = (_lookup(pstart - start, cls) + inv).astype(i32)
    return _lookup(cls_ea, blk_cls), _lookup(cls_eb, blk_cls), blk_groups, tok_buf.reshape(-1), pos


def _layer(x, p, alpha):
    (w_in, b_in, lam_re, lam_im, log_dt, ssm_b_re, ssm_b_im, ssm_c_re, ssm_c_im,
     ssm_d, w_glu, b_glu, conv_w, conv_b, ln_c_g, ln_c_b, w_cout, b_cout, w_out, b_out,
     ln1_g, ln1_b, w_rg, b_rg, w_re, b_re, w_gate, w_up, w_down, ln2_g, ln2_b) = p
    bsz, seq, _ = x.shape
    n_tok = bsz * seq
    assert seq % CHUNK == 0
    nj = seq // CHUNK
    n_rows = bsz * nj
    assert nj & (nj - 1) == 0, "chunks per sequence must be a power of two"
    n_steps = max(1, int(math.log2(nj)))
    assert n_steps <= 8

    n_ucv = D_SSM + 2 * D_CONV
    row = lambda a: a.reshape(1, -1).astype(F32)
    rg = min(512, n_rows)

    ut, vt, xt = _inproj(x.reshape(n_tok, D_MODEL), bsz, seq,
                         w_in[:, :D_SSM].T.astype(BF16), b_in[:D_SSM].reshape(-1, 1).astype(F32),
                         w_in[:, D_SSM:n_ucv].astype(BF16), row(b_in[D_SSM:n_ucv]))

    mzt, wt_state, coef = _s5_chunk_weights(lam_re, lam_im, log_dt, ssm_b_re, ssm_b_im,
                                            ssm_c_re, ssm_c_im, n_steps)
    dcol = jnp.tile(ssm_d.astype(F32).reshape(N_SSM_GROUPS, 1, SSM_GROUP), (1, CHUNK, 1))
    dcol = dcol.reshape(N_SSM_GROUPS, CHUNK_F, 1)
    yst = _s5(ut.reshape(CHUNK, N_SSM_GROUPS, SSM_GROUP, n_rows), mzt, wt_state, coef, dcol, nj, n_steps)
    yst = yst.reshape(CHUNK, D_SSM, n_rows)

    tl = min(512, seq)
    c_act = _conv(vt, bsz, seq, conv_w.reshape(CONV_WIDTH, D_CONV).astype(F32),
                  row(conv_b), row(ln_c_g), row(ln_c_b), tl)

    w_r = jnp.zeros((D_MODEL, ROUTE_LANES), F32)
    w_r = w_r.at[:, :N_GROUPS].set(w_rg).at[:, N_GROUPS:N_GROUPS + N_EXPERTS].set(w_re)
    b_r = jnp.zeros((1, ROUTE_LANES), F32)
    b_r = b_r.at[0, :N_GROUPS].set(b_rg).at[0, N_GROUPS:N_GROUPS + N_EXPERTS].set(b_re)
    w_r_hi = w_r.astype(BF16)
    w_r_lo = (w_r - w_r_hi.astype(F32)).astype(BF16)
    wts = (w_in[:, n_ucv:].astype(BF16), row(b_in[n_ucv:]),
           w_glu.astype(BF16), row(b_glu),
           w_cout.astype(BF16), row(b_cout), w_out.astype(BF16), row(b_out),
           row(ln1_g), row(ln1_b), w_r_hi, w_r_lo, b_r)
    ht, routet = _mix(xt, yst, c_act, wts, rg, alpha)

    tm_moe = 128
    cls = routet[:, :, 4].reshape(n_tok).astype(jnp.int32)
    blk_ea, blk_eb, blk_groups, tok_rows, pos = _dispatch(cls, tm_moe)
    w_gu = jnp.concatenate([w_gate, w_up], axis=-1).astype(BF16)
    y_buf = _moe(blk_ea, blk_eb, blk_groups, tok_rows, ht.reshape(n_tok * TOKEN_SUBLANES, LANES), w_gu,
                 w_down.astype(BF16), tm_moe)
    tt = CHUNK * CHUNK
    n_r = tt // CHUNK
    pos_rows = pos.reshape(CHUNK, n_rows // n_r, n_r).transpose(1, 0, 2).reshape(n_tok)
    out = _combine(pos_rows, ht, routet, y_buf, row(ln2_g), row(ln2_b), tt, alpha)
    return out.reshape(bsz, seq, D_MODEL)


def kernel(x, w_in, b_in, lam_re, lam_im, log_dt, ssm_b_re, ssm_b_im, ssm_c_re, ssm_c_im, ssm_d, w_glu, b_glu, conv_w, conv_b, ln_c_g, ln_c_b, w_cout, b_cout, w_out, b_out, ln1_g, ln1_b, w_route_group, b_route_group, w_route_expert, b_route_expert, w_gate, w_up, w_down, ln2_g, ln2_b):
    params = (w_in, b_in, lam_re, lam_im, log_dt, ssm_b_re, ssm_b_im, ssm_c_re, ssm_c_im,
              ssm_d, w_glu, b_glu, conv_w, conv_b, ln_c_g, ln_c_b, w_cout, b_cout, w_out, b_out,
              ln1_g, ln1_b, w_route_group, b_route_group, w_route_expert, b_route_expert,
              w_gate, w_up, w_down, ln2_g, ln2_b)
    depth = w_in.shape[0]
    alpha = (2.0 * depth) ** 0.25
    h = x
    for i in range(depth):
        h = _layer(h, tuple(p[i] for p in params), alpha)
    return h
```

```python
import functools
import math

import jax
import jax.numpy as jnp
from jax import lax
from jax.experimental import pallas as pl
from jax.experimental.pallas import tpu as pltpu

F32 = jnp.float32
BF16 = jnp.bfloat16
HIGHEST = lax.Precision.HIGHEST

D_MODEL = 1024
D_SSM = 512
SSM_GROUP = 16
N_SSM_GROUPS = D_SSM // SSM_GROUP
SSM_STATE = 64
D_CONV = 512
CONV_WIDTH = 31
N_GROUPS = 4
N_EXP_PER_GROUP = 8
N_EXPERTS = N_GROUPS * N_EXP_PER_GROUP
TOP_K_INNER = 2
N_PAIRS = N_EXP_PER_GROUP * (N_EXP_PER_GROUP - 1) // 2
N_CLASSES = N_GROUPS * N_PAIRS
D_FF_EXPERT = D_MODEL // 4
LN_EPS = 1e-5

CHUNK = 16
CHUNK_F = CHUNK * SSM_GROUP
STATE_F = 2 * SSM_STATE
ROUTE_LANES = 128
HALO = 32
HALO_CHUNKS = 8
SUBLANES = 8
LANES = 128
TOKEN_SUBLANES = D_MODEL // LANES
GATHER_SLOTS = 3
ROW_GROUP = 8
VMEM_LIMIT = 56 * 1024 * 1024

_NT = (((1,), (1,)), ((), ()))
_TN = (((0,), (0,)), ((), ()))


def _layer_norm(x, g, b):
    mu = jnp.mean(x, axis=-1, keepdims=True)
    xc = x - mu
    var = jnp.mean(xc * xc, axis=-1, keepdims=True)
    return xc * lax.rsqrt(var + LN_EPS) * g + b


def _inproj_kernel(*refs, nj):
    n_cb = D_MODEL // LANES
    x_refs = refs[:n_cb]
    wut_ref, but_ref, wcv_ref, bcv_ref, ut_ref, v_ref, xt_ref = refs[n_cb:]
    taus_per_dot = 4
    for q in range(CHUNK // taus_per_dot):
        taus = range(q * taus_per_dot, (q + 1) * taus_per_dot)
        x = jnp.concatenate(
            [jnp.concatenate([xr[pl.ds(t, nj, stride=CHUNK), :] for xr in x_refs], axis=1) for t in taus],
            axis=0).astype(BF16)
        ut = lax.dot_general(wut_ref[...], x, _NT, preferred_element_type=F32) + but_ref[...]
        p = jnp.dot(x, wcv_ref[...], preferred_element_type=F32) + bcv_ref[...]
        v = p[:, :D_CONV] * jax.nn.sigmoid(p[:, D_CONV:])
        for k, t in enumerate(taus):
            rows = slice(k * nj, (k + 1) * nj)
            ut_ref[t] = ut[:, rows].astype(BF16)
            v_ref[t] = v[rows]
            xt_ref[t] = x[rows]


def _inproj(x2d, bsz, seq, w_ut, b_ut, w_cv, b_cv):
    nj = seq // CHUNK
    n_rows = bsz * nj
    n_cb = D_MODEL // LANES
    const = lambda b: (0, 0)
    col_block = lambda c: pl.BlockSpec((seq, LANES), lambda b: (b, c))
    return pl.pallas_call(
        functools.partial(_inproj_kernel, nj=nj),
        grid=(bsz,),
        in_specs=[col_block(c) for c in range(n_cb)] + [
            pl.BlockSpec(w_ut.shape, const),
            pl.BlockSpec(b_ut.shape, const),
            pl.BlockSpec(w_cv.shape, const),
            pl.BlockSpec(b_cv.shape, const),
        ],
        out_specs=[
            pl.BlockSpec((CHUNK, D_SSM, nj), lambda b: (0, 0, b)),
            pl.BlockSpec((CHUNK, nj, D_CONV), lambda b: (0, b, 0)),
            pl.BlockSpec((CHUNK, nj, D_MODEL), lambda b: (0, b, 0)),
        ],
        out_shape=[
            jax.ShapeDtypeStruct((CHUNK, D_SSM, n_rows), BF16),
            jax.ShapeDtypeStruct((CHUNK, n_rows, D_CONV), F32),
            jax.ShapeDtypeStruct((CHUNK, n_rows, D_MODEL), BF16),
        ],
        compiler_params=pltpu.CompilerParams(
            dimension_semantics=("parallel",), vmem_limit_bytes=VMEM_LIMIT),
        name="inproj",
    )(*([x2d] * n_cb), w_ut, b_ut, w_cv, b_cv)


def _s5_prep_kernel(ca_ref, bb_ref, pc_ref, mzt_ref):
    bb = bb_ref[0]
    k_all = jnp.dot(ca_ref[0, :CHUNK_F, :], bb, precision=HIGHEST, preferred_element_type=F32)
    mzt_ref[0, :CHUNK_F, :] = jnp.zeros((CHUNK_F, CHUNK_F), F32)
    bb_sw = pltpu.roll(bb, SSM_STATE, axis=0)
    for t_in in range(CHUNK):
        cols = slice(t_in * SSM_GROUP, (t_in + 1) * SSM_GROUP)
        mzt_ref[0, t_in * SSM_GROUP:CHUNK_F, cols] = k_all[:(CHUNK - t_in) * SSM_GROUP, :]
        mzt_ref[0, CHUNK_F:, cols] = (pc_ref[0, :, t_in:t_in + 1] * bb
                                      + pc_ref[0, :, CHUNK + t_in:CHUNK + t_in + 1] * bb_sw)


def _s5_prep(ca, bb, pcols):
    g = ca.shape[0]
    return pl.pallas_call(
        _s5_prep_kernel,
        grid=(g,),
        in_specs=[pl.BlockSpec((1,) + ca.shape[1:], lambda i: (i, 0, 0)),
                  pl.BlockSpec((1,) + bb.shape[1:], lambda i: (i, 0, 0)),
                  pl.BlockSpec((1,) + pcols.shape[1:], lambda i: (i, 0, 0))],
        out_specs=pl.BlockSpec((1, CHUNK_F + STATE_F, CHUNK_F), lambda i: (i, 0, 0)),
        out_shape=jax.ShapeDtypeStruct((g, CHUNK_F + STATE_F, CHUNK_F), F32),
        compiler_params=pltpu.CompilerParams(dimension_semantics=("parallel",)),
        name="s5prep",
    )(ca, bb, pcols)


def _s5_chunk_weights(lam_re, lam_im, log_dt, b_re, b_im, c_re, c_im, n_steps):
    f32 = F32
    lr, li = lam_re.astype(f32), lam_im.astype(f32)
    dt = jnp.exp(log_dt.astype(f32))[:, None]
    mag = jnp.exp(lr * dt)
    ar, ai = mag * jnp.cos(li * dt), mag * jnp.sin(li * dt)
    den = lr * lr + li * li
    fr = ((ar - 1.0) * lr + ai * li) / den
    fi = (ai * lr - (ar - 1.0) * li) / den
    br, bi = b_re.astype(f32), b_im.astype(f32)
    bbr = fr[..., None] * br - fi[..., None] * bi
    bbi = fr[..., None] * bi + fi[..., None] * br
    cr, ci = c_re.astype(f32), c_im.astype(f32)

    kk_pow = jnp.arange(CHUNK + 1, dtype=f32)[:, None, None]
    mag_k = jnp.exp(kk_pow * (lr * dt)[None])
    pw_r = mag_k * jnp.cos(kk_pow * (li * dt)[None])
    pw_i = mag_k * jnp.sin(kk_pow * (li * dt)[None])

    ca_r = cr[None] * pw_r[:, :, None, :] - ci[None] * pw_i[:, :, None, :]
    ca_i = cr[None] * pw_i[:, :, None, :] + ci[None] * pw_r[:, :, None, :]
    ca = jnp.concatenate([ca_r, -ca_i], axis=-1).transpose(1, 0, 2, 3)
    ca = ca.reshape(N_SSM_GROUPS, (CHUNK + 1) * SSM_GROUP, STATE_F)
    bb = jnp.concatenate([bbr, bbi], axis=1)
    rp_r, rp_i = pw_r[CHUNK - 1::-1][:CHUNK], pw_i[CHUNK - 1::-1][:CHUNK]
    pcols = jnp.concatenate([jnp.concatenate([rp_r, rp_r], axis=-1).transpose(1, 2, 0),
                             jnp.concatenate([-rp_i, rp_i], axis=-1).transpose(1, 2, 0)], axis=-1)
    mzt = _s5_prep(ca, bb, pcols)
    wt = ca[:, SSM_GROUP:, :]

    sr, si = pw_r[CHUNK], pw_i[CHUNK]
    c1, c2 = [], []
    for _ in range(n_steps):
        c1.append(jnp.concatenate([sr, sr], axis=-1)[None])
        c2.append(jnp.concatenate([-si, si], axis=-1)[None])
        sr, si = sr * sr - si * si, 2.0 * sr * si
    pad = jnp.zeros((8 - n_steps, N_SSM_GROUPS, STATE_F), f32)
    coef = jnp.concatenate(c1 + [pad] + c2 + [pad], axis=0).transpose(1, 2, 0)
    return mzt.astype(BF16), wt.astype(BF16), coef


def _s5_kernel(ut_ref, mzt_ref, wt_ref, coef_ref, dcol_ref, o_ref, *, nj, n_steps):
    ut = jnp.concatenate([ut_ref[t] for t in range(CHUNK)], axis=0)
    r = jnp.dot(mzt_ref[0], ut, preferred_element_type=F32)
    y0 = r[:CHUNK_F]
    t = r[CHUNK_F:]
    j = lax.broadcasted_iota(jnp.int32, t.shape, 1) & (nj - 1)
    for m in range(n_steps):
        d = 1 << m
        sh = jnp.where(j >= d, pltpu.roll(t, d, axis=1), 0.0)
        p1 = coef_ref[0, :, m:m + 1]
        p2 = coef_ref[0, :, 8 + m:9 + m]
        t = t + p1 * sh + p2 * pltpu.roll(sh, SSM_STATE, axis=0)
    s_prev = jnp.where(j >= 1, pltpu.roll(t, 1, axis=1), 0.0)
    y = y0 + jnp.dot(wt_ref[0], s_prev.astype(BF16), preferred_element_type=F32)
    ys = jax.nn.gelu(y + dcol_ref[0] * ut.astype(F32)).astype(BF16)
    for t_out in range(CHUNK):
        o_ref[t_out] = ys[t_out * SSM_GROUP:(t_out + 1) * SSM_GROUP]


def _s5(ut4, mzt, wt, coef, dcol, nj, n_steps):
    _, g, _, n_rows = ut4.shape
    blk = pl.BlockSpec((CHUNK, None, SSM_GROUP, n_rows), lambda i: (0, i, 0, 0))
    return pl.pallas_call(
        functools.partial(_s5_kernel, nj=nj, n_steps=n_steps),
        grid=(g,),
        in_specs=[
            blk,
            pl.BlockSpec((1, CHUNK_F + STATE_F, CHUNK_F), lambda i: (i, 0, 0)),
            pl.BlockSpec((1, CHUNK_F, STATE_F), lambda i: (i, 0, 0)),
            pl.BlockSpec((1, STATE_F, 16), lambda i: (i, 0, 0)),
            pl.BlockSpec((1, CHUNK_F, 1), lambda i: (i, 0, 0)),
        ],
        out_specs=blk,
        out_shape=jax.ShapeDtypeStruct(ut4.shape, BF16),
        compiler_params=pltpu.CompilerParams(
            dimension_semantics=("parallel",), vmem_limit_bytes=VMEM_LIMIT),
        name="s5",
    )(ut4, mzt, wt, coef, dcol)


def _conv_kernel(v_ref, vh_ref, cw_ref, cb_ref, g_ref, b_ref, c_ref, vext_ref, shift_ref, conv_ref, *, tl, sub):
    i = pl.program_id(1)
    n_ext = tl + HALO
    n_chunks = tl // CHUNK
    n_hist = HALO // CHUNK
    blocks = [slice(c * LANES, (c + 1) * LANES) for c in range(D_CONV // LANES)]
    for t in range(CHUNK):
        for c, lanes in enumerate(blocks):
            hist = vh_ref[t, HALO_CHUNKS - n_hist:HALO_CHUNKS, lanes]
            vext_ref[c, pl.ds(t, n_hist, stride=CHUNK), :] = jnp.where(i > 0, hist, 0.0)
            vext_ref[c, pl.ds(HALO + t, n_chunks, stride=CHUNK), :] = v_ref[t, :, lanes]
    for m in range(SUBLANES):
        shift_ref[m, :, 0:n_ext - m, :] = vext_ref[:, m:n_ext, :]
    off = HALO - (CONV_WIDTH - 1)
    for c, lanes in enumerate(blocks):
        for s in range(tl // sub):
            acc = jnp.zeros((sub, LANES), F32) + cb_ref[:, lanes]
            for k in range(CONV_WIDTH):
                q, m = divmod(off + k, SUBLANES)
                r0 = s * sub + q * SUBLANES
                acc = acc + cw_ref[k:k + 1, lanes] * shift_ref[m, c, r0:r0 + sub, :]
            conv_ref[c, s * sub:(s + 1) * sub, :] = acc
    xs = [conv_ref[c] for c in range(len(blocks))]
    mu = sum(jnp.sum(x, axis=-1, keepdims=True) for x in xs) * (1.0 / D_CONV)
    xc = [x - mu for x in xs]
    var = sum(jnp.sum(x * x, axis=-1, keepdims=True) for x in xc) * (1.0 / D_CONV)
    inv = lax.rsqrt(var + LN_EPS)
    for c, lanes in enumerate(blocks):
        conv_ref[c] = jax.nn.silu(xc[c] * inv * g_ref[:, lanes] + b_ref[:, lanes])
    for t in range(CHUNK):
        for c, lanes in enumerate(blocks):
            c_ref[t, :, lanes] = conv_ref[c, pl.ds(t, n_chunks, stride=CHUNK), :].astype(BF16)


def _conv(vt, b, l, conv_w, conv_b, ln_g, ln_b, tl):
    sub = min(64, tl)
    n_chunks = tl // CHUNK
    assert n_chunks % HALO_CHUNKS == 0
    tiles = l // tl
    const = lambda bi, i: (0, 0)
    row_spec = pl.BlockSpec((CHUNK, n_chunks, D_CONV), lambda bi, i: (0, bi * tiles + i, 0))
    halo_spec = pl.BlockSpec(
        (CHUNK, HALO_CHUNKS, D_CONV),
        lambda bi, i: (0, jnp.maximum((bi * tiles + i) * (n_chunks // HALO_CHUNKS) - 1, 0), 0))
    return pl.pallas_call(
        functools.partial(_conv_kernel, tl=tl, sub=sub),
        grid=(b, l // tl),
        in_specs=[row_spec, halo_spec,
                  pl.BlockSpec(conv_w.shape, const), pl.BlockSpec(conv_b.shape, const),
                  pl.BlockSpec(ln_g.shape, const), pl.BlockSpec(ln_b.shape, const)],
        out_specs=row_spec,
        out_shape=jax.ShapeDtypeStruct((CHUNK, b * l // CHUNK, D_CONV), BF16),
        scratch_shapes=[
            pltpu.VMEM((D_CONV // LANES, tl + HALO, LANES), F32),
            pltpu.VMEM((SUBLANES, D_CONV // LANES, tl + HALO, LANES), F32),
            pltpu.VMEM((D_CONV // LANES, tl, LANES), F32),
        ],
        compiler_params=pltpu.CompilerParams(
            dimension_semantics=("parallel", "arbitrary"), vmem_limit_bytes=VMEM_LIMIT),
        name="conv",
    )(vt, vt, conv_w, conv_b, ln_g, ln_b)


def _mix_kernel(x_ref, yst_ref, c_ref,
                wg_ref, bg_ref, wglu_ref, bglu_ref, wco_ref, bco_ref,
                wo_ref, bo_ref, l1g_ref, l1b_ref, wrh_ref, wrl_ref, br_ref,
                h_ref, route_ref, *, alpha):
    c_out = jnp.dot(c_ref[...], wco_ref[...], preferred_element_type=F32) + bco_ref[...]

    glu = lax.dot_general(yst_ref[...], wglu_ref[...], _TN, preferred_element_type=F32) + bglu_ref[...]
    s_out = glu[:, :D_MODEL] * jax.nn.sigmoid(glu[:, D_MODEL:])

    xb = x_ref[...]
    gates = jnp.dot(xb, wg_ref[...], preferred_element_type=F32) + bg_ref[...]
    merged = jax.nn.sigmoid(gates[:, :D_MODEL]) * s_out + jax.nn.sigmoid(gates[:, D_MODEL:]) * c_out
    mix = jnp.dot(merged.astype(BF16), wo_ref[...], preferred_element_type=F32) + bo_ref[...]
    h = _layer_norm(alpha * xb.astype(F32) + mix, l1g_ref[...], l1b_ref[...])
    rg = h.shape[0]
    for s in range(TOKEN_SUBLANES):
        h_ref[pl.ds(s, rg, stride=TOKEN_SUBLANES), :] = h[:, s * LANES:(s + 1) * LANES]

    h_hi = h.astype(BF16)
    h_lo = (h - h_hi.astype(F32)).astype(BF16)
    z = (jnp.dot(h_hi, wrh_ref[...], preferred_element_type=F32)
         + jnp.dot(h_lo, wrh_ref[...], preferred_element_type=F32)
         + jnp.dot(h_hi, wrl_ref[...], preferred_element_type=F32)) + br_ref[...]
    lane = lax.broadcasted_iota(jnp.int32, z.shape, 1)
    lanef = lane.astype(F32)
    neg = jnp.float32(-jnp.inf)
    big = jnp.float32(1e9)
    gmask = lane < N_GROUPS
    zg = jnp.where(gmask, z, neg)
    gmax = jnp.max(zg, axis=1, keepdims=True)
    gsel = jnp.min(jnp.where(gmask & (zg == gmax), lanef, big), axis=1, keepdims=True)
    p_group = 1.0 / jnp.sum(jnp.where(gmask, jnp.exp(zg - gmax), 0.0), axis=1, keepdims=True)
    lo = N_GROUPS + N_EXP_PER_GROUP * gsel
    emask = (lanef >= lo) & (lanef < lo + N_EXP_PER_GROUP)
    ze = jnp.where(emask, z, neg)
    v1 = jnp.max(ze, axis=1, keepdims=True)
    i1 = jnp.min(jnp.where(emask & (ze == v1), lanef, big), axis=1, keepdims=True)
    emask2 = emask & (lanef != i1)
    ze2 = jnp.where(emask2, z, neg)
    v2 = jnp.max(ze2, axis=1, keepdims=True)
    i2 = jnp.min(jnp.where(emask2 & (ze2 == v2), lanef, big), axis=1, keepdims=True)
    e2 = jnp.exp(v2 - v1)
    inv = 1.0 / (1.0 + e2)
    g1 = p_group * inv
    g2 = p_group * (e2 * inv)
    a = jnp.minimum(i1, i2) - lo
    b = jnp.maximum(i1, i2) - lo
    cls = gsel * N_PAIRS + (a * (2 * N_EXP_PER_GROUP - 1 - a)) * 0.5 + (b - a - 1.0)
    out = jnp.where(lane == 0, i1 - N_GROUPS, 0.0)
    out = jnp.where(lane == 1, i2 - N_GROUPS, out)
    out = jnp.where(lane == 2, g1, out)
    out = jnp.where(lane == 3, g2, out)
    out = jnp.where(lane == 4, cls, out)
    route_ref[...] = out


def _mix(xt, yst, ct, wts, rg, alpha):
    n_rows = xt.shape[1]

    def tau_spec(f):
        return pl.BlockSpec((None, rg, f), lambda i, t: (t, i, 0))

    def full_spec(a):
        nd = a.ndim
        return pl.BlockSpec(a.shape, lambda i, t: (0,) * nd)

    return pl.pallas_call(
        functools.partial(_mix_kernel, alpha=alpha),
        grid=(n_rows // rg, CHUNK),
        in_specs=[tau_spec(D_MODEL),
                  pl.BlockSpec((None, D_SSM, rg), lambda i, t: (t, 0, i)),
                  tau_spec(D_CONV)]
                 + [full_spec(a) for a in wts],
        out_specs=[pl.BlockSpec((None, rg * TOKEN_SUBLANES, LANES), lambda i, t: (t, i, 0)),
                   tau_spec(ROUTE_LANES)],
        out_shape=[
            jax.ShapeDtypeStruct((CHUNK, n_rows * TOKEN_SUBLANES, LANES), F32),
            jax.ShapeDtypeStruct((CHUNK, n_rows, ROUTE_LANES), F32),
        ],
        compiler_params=pltpu.CompilerParams(
            dimension_semantics=("parallel", "arbitrary"), vmem_limit_bytes=VMEM_LIMIT),
        name="mix",
    )(xt, yst, ct, *wts)


def _gather_rows(idx_ref, base, src_hbm, dst, sem, n, rows_per):
    def issue(r, carry):
        src0 = pl.multiple_of(idx_ref[base + r] * rows_per, rows_per)
        dst0 = pl.multiple_of(r * rows_per, rows_per)
        pltpu.make_async_copy(src_hbm.at[pl.ds(src0, rows_per)], dst.at[pl.ds(dst0, rows_per)], sem).start()
        return carry
    lax.fori_loop(0, n, issue, 0, unroll=8)


def _gather_row_groups(idx_ref, base, src_hbm, dst, sem, n_groups, rows_per):
    def issue(g, carry):
        for k in range(ROW_GROUP):
            r = g * ROW_GROUP + k
            src0 = pl.multiple_of(idx_ref[base + r] * rows_per, rows_per)
            dst0 = pl.multiple_of(r * rows_per, rows_per)
            pltpu.make_async_copy(src_hbm.at[pl.ds(src0, rows_per)], dst.at[pl.ds(dst0, rows_per)], sem).start()
        return carry
    lax.fori_loop(0, n_groups, issue, 0)


def _wait_row_groups(src_hbm, dst, sem, n_groups, rows_per):
    rows = ROW_GROUP * rows_per

    def wait(g, carry):
        pltpu.make_async_copy(src_hbm.at[pl.ds(0, rows)], dst.at[pl.ds(0, rows)], sem).wait()
        return carry
    lax.fori_loop(0, n_groups, wait, 0)


def _wait_rows(src_hbm, dst, sem):
    pltpu.make_async_copy(src_hbm.at[pl.ds(0, dst.shape[0])], dst, sem).wait()


def _lane_blocks(ref2d, n, rows_per):
    return jnp.concatenate([ref2d[pl.ds(s, n, stride=rows_per), :] for s in range(rows_per)], axis=1)


def _expert(xb, wgu, wd):
    gu = jnp.dot(xb, wgu, preferred_element_type=F32)
    hid = jax.nn.silu(gu[:, :D_FF_EXPERT]) * gu[:, D_FF_EXPERT:]
    return jnp.dot(hid.astype(BF16), wd, preferred_element_type=F32)


def _moe_kernel(ea_ref, eb_ref, groups_ref, tok_ref, h_hbm, gate_ref, wgu_a_ref, wgu_b_ref, wd_a_ref, wd_b_ref,
                y_ref, xbuf, sem, *, tm):
    del ea_ref, eb_ref
    i = pl.program_id(0)
    n = pl.num_programs(0)
    slot = i % GATHER_SLOTS

    def gather(blk):
        n_groups = groups_ref[jnp.minimum(blk, n - 1)]

        @pl.when((blk < n) & (n_groups > 0))
        def _():
            s = blk % GATHER_SLOTS
            _gather_row_groups(tok_ref, blk * tm, h_hbm, xbuf.at[s], sem.at[s], n_groups, TOKEN_SUBLANES)

    @pl.when(i == 0)
    def _():
        xbuf[...] = jnp.zeros_like(xbuf)
        for blk in range(GATHER_SLOTS - 1):
            gather(jnp.int32(blk))

    @pl.when(groups_ref[i] > 0)
    def _():
        _wait_row_groups(h_hbm, xbuf.at[slot], sem.at[slot], groups_ref[i], TOKEN_SUBLANES)
        xb = _lane_blocks(xbuf.at[slot], tm, TOKEN_SUBLANES).astype(BF16)
        gate = gate_ref[0]
        y = (gate[:, 0:1] * _expert(xb, wgu_a_ref[0], wd_a_ref[0])
             + gate[:, 1:2] * _expert(xb, wgu_b_ref[0], wd_b_ref[0]))
        for s in range(TOKEN_SUBLANES):
            y_ref[pl.ds(s, tm, stride=TOKEN_SUBLANES), :] = y[:, s * LANES:(s + 1) * LANES]

    @pl.when(groups_ref[i] == 0)
    def _():
        y_ref[...] = jnp.zeros_like(y_ref)

    gather(i + GATHER_SLOTS - 1)


def _moe(blk_ea, blk_eb, blk_groups, tok_rows, h2, gate_rows, w_gu, w_d, tm):
    n_blocks = blk_ea.shape[0]
    grid_spec = pltpu.PrefetchScalarGridSpec(
        num_scalar_prefetch=4,
        grid=(n_blocks,),
        in_specs=[
            pl.BlockSpec(memory_space=pl.ANY),
            pl.BlockSpec((1, tm, TOP_K_INNER), lambda i, ea, eb, groups, tok: (i, 0, 0)),
            pl.BlockSpec((1, D_MODEL, 2 * D_FF_EXPERT), lambda i, ea, eb, groups, tok: (ea[i], 0, 0)),
            pl.BlockSpec((1, D_MODEL, 2 * D_FF_EXPERT), lambda i, ea, eb, groups, tok: (eb[i], 0, 0)),
            pl.BlockSpec((1, D_FF_EXPERT, D_MODEL), lambda i, ea, eb, groups, tok: (ea[i], 0, 0)),
            pl.BlockSpec((1, D_FF_EXPERT, D_MODEL), lambda i, ea, eb, groups, tok: (eb[i], 0, 0)),
        ],
        out_specs=pl.BlockSpec((tm * TOKEN_SUBLANES, LANES), lambda i, ea, eb, groups, tok: (i, 0)),
        scratch_shapes=[pltpu.VMEM((GATHER_SLOTS, tm * TOKEN_SUBLANES, LANES), F32),
                        pltpu.SemaphoreType.DMA((GATHER_SLOTS,))],
    )
    return pl.pallas_call(
        functools.partial(_moe_kernel, tm=tm),
        grid_spec=grid_spec,
        out_shape=jax.ShapeDtypeStruct((n_blocks * tm * TOKEN_SUBLANES, LANES), F32),
        compiler_params=pltpu.CompilerParams(
            dimension_semantics=("arbitrary",), vmem_limit_bytes=VMEM_LIMIT),
        name="moe",
    )(blk_ea, blk_eb, blk_groups, tok_rows, h2, gate_rows, w_gu, w_gu, w_d, w_d)


def _combine_kernel(pos_ref, h_ref, y_hbm, g_ref, b_ref, o_ref, buf, res_ref, sem, *, tt, alpha):
    i = pl.program_id(0)
    n = pl.num_programs(0)
    slot = i % 2
    n_r = tt // CHUNK

    @pl.when(i == 0)
    def _():
        _gather_rows(pos_ref, 0, y_hbm, buf.at[0], sem.at[0], tt, TOKEN_SUBLANES)

    @pl.when(i + 1 < n)
    def _():
        _gather_rows(pos_ref, (i + 1) * tt, y_hbm, buf.at[1 - slot], sem.at[1 - slot], tt, TOKEN_SUBLANES)

    _wait_rows(y_hbm, buf.at[slot], sem.at[slot])
    moe = _lane_blocks(buf.at[slot], tt, TOKEN_SUBLANES)
    h = jnp.concatenate([_lane_blocks(h_ref.at[t], n_r, TOKEN_SUBLANES) for t in range(CHUNK)], axis=0)
    res = _layer_norm(alpha * h + moe, g_ref[...], b_ref[...])
    n_cb = D_MODEL // LANES
    for c in range(n_cb):
        res_ref[c] = res[:, c * LANES:(c + 1) * LANES]
    for r in range(n_r):
        for c in range(n_cb):
            o_ref[r * CHUNK:(r + 1) * CHUNK, c * LANES:(c + 1) * LANES] = res_ref[c, pl.ds(r, CHUNK, stride=n_r), :]


def _combine(pos_rows, ht, y_buf, ln_g, ln_b, tt, alpha):
    n_tok = pos_rows.shape[0]
    n_tiles = n_tok // tt
    n_r = tt // CHUNK
    grid_spec = pltpu.PrefetchScalarGridSpec(
        num_scalar_prefetch=1,
        grid=(n_tiles,),
        in_specs=[
            pl.BlockSpec((CHUNK, n_r * TOKEN_SUBLANES, LANES), lambda i, pos: (0, i, 0)),
            pl.BlockSpec(memory_space=pl.ANY),
            pl.BlockSpec((1, D_MODEL), lambda i, pos: (0, 0)),
            pl.BlockSpec((1, D_MODEL), lambda i, pos: (0, 0)),
        ],
        out_specs=pl.BlockSpec((tt, D_MODEL), lambda i, pos: (i, 0)),
        scratch_shapes=[
            pltpu.VMEM((2, tt * TOKEN_SUBLANES, LANES), F32),
            pltpu.VMEM((D_MODEL // LANES, tt, LANES), F32),
            pltpu.SemaphoreType.DMA((2,)),
        ],
    )
    return pl.pallas_call(
        functools.partial(_combine_kernel, tt=tt, alpha=alpha),
        grid_spec=grid_spec,
        out_shape=jax.ShapeDtypeStruct((n_tok, D_MODEL), F32),
        compiler_params=pltpu.CompilerParams(
            dimension_semantics=("arbitrary",), vmem_limit_bytes=VMEM_LIMIT),
        name="combine",
    )(pos_rows, ht, y_buf, ln_g, ln_b)


def _class_experts():
    ea, eb = [], []
    for g in range(N_GROUPS):
        for a in range(N_EXP_PER_GROUP):
            for b in range(a + 1, N_EXP_PER_GROUP):
                ea.append(g * N_EXP_PER_GROUP + a)
                eb.append(g * N_EXP_PER_GROUP + b)
    return jnp.asarray(ea, jnp.int32), jnp.asarray(eb, jnp.int32)


def _lookup(table, idx):
    n = table.shape[0]
    hit = idx[..., None] == jnp.arange(n, dtype=idx.dtype)
    return jnp.sum(jnp.where(hit, table, 0), axis=-1).astype(table.dtype)


def _dispatch(cls, gate_a, gate_b, tm):
    n_tok = cls.shape[0]
    i32 = jnp.int32
    counts = jnp.sum(cls[:, None] == jnp.arange(N_CLASSES, dtype=i32)[None, :], axis=0).astype(i32)
    pcounts = (counts + tm - 1) // tm * tm
    pend = jnp.cumsum(pcounts)
    pstart = pend - pcounts
    n_blocks = -(-n_tok // tm) + N_CLASSES
    n_rows = n_blocks * tm
    blk_start = jnp.arange(n_blocks, dtype=i32) * tm
    blk_cls = jnp.minimum(jnp.sum(blk_start[:, None] >= pend[None, :], axis=1), N_CLASSES - 1).astype(i32)
    cls_ea, cls_eb = _class_experts()
    blk_first = blk_start - _lookup(pstart, blk_cls)
    blk_groups = ((jnp.clip(_lookup(counts, blk_cls) - blk_first, 0, tm) + ROW_GROUP - 1) // ROW_GROUP).astype(i32)
    pad_end = jnp.cumsum(pcounts - counts)
    pad_cls = jnp.sum(jnp.arange(n_rows - n_tok, dtype=i32)[:, None] >= pad_end[None, :], axis=1).astype(i32)
    key = jnp.concatenate([2 * cls, 2 * pad_cls + 1])
    zeros = jnp.zeros((n_rows - n_tok,), gate_a.dtype)
    key_s, tok_rows, ga_rows, gb_rows = lax.sort(
        (key, jnp.concatenate([jnp.arange(n_tok, dtype=i32), jnp.zeros((n_rows - n_tok,), i32)]),
         jnp.concatenate([gate_a, zeros]), jnp.concatenate([gate_b, zeros])),
        num_keys=1, is_stable=True)
    is_token = (key_s % 2) == 0
    _, pos = lax.sort((jnp.where(is_token, tok_rows, n_tok), jnp.arange(n_rows, dtype=i32)), num_keys=1)
    gate_rows = jnp.stack([ga_rows, gb_rows], axis=-1).reshape(n_blocks, tm, TOP_K_INNER)
    return _lookup(cls_ea, blk_cls), _lookup(cls_eb, blk_cls), blk_groups, tok_rows, gate_rows, pos[:n_tok]


def _layer(x, p, alpha):
    (w_in, b_in, lam_re, lam_im, log_dt, ssm_b_re, ssm_b_im, ssm_c_re, ssm_c_im,
     ssm_d, w_glu, b_glu, conv_w, conv_b, ln_c_g, ln_c_b, w_cout, b_cout, w_out, b_out,
     ln1_g, ln1_b, w_rg, b_rg, w_re, b_re, w_gate, w_up, w_down, ln2_g, ln2_b) = p
    bsz, seq, _ = x.shape
    n_tok = bsz * seq
    assert seq % CHUNK == 0
    nj = seq // CHUNK
    n_rows = bsz * nj
    assert nj & (nj - 1) == 0, "chunks per sequence must be a power of two"
    n_steps = max(1, int(math.log2(nj)))
    assert n_steps <= 8

    n_ucv = D_SSM + 2 * D_CONV
    row = lambda a: a.reshape(1, -1).astype(F32)
    rg = min(512, n_rows)

    ut, vt, xt = _inproj(x.reshape(n_tok, D_MODEL), bsz, seq,
                         w_in[:, :D_SSM].T.astype(BF16), b_in[:D_SSM].reshape(-1, 1).astype(F32),
                         w_in[:, D_SSM:n_ucv].astype(BF16), row(b_in[D_SSM:n_ucv]))

    mzt, wt_state, coef = _s5_chunk_weights(lam_re, lam_im, log_dt, ssm_b_re, ssm_b_im,
                                            ssm_c_re, ssm_c_im, n_steps)
    dcol = jnp.tile(ssm_d.astype(F32).reshape(N_SSM_GROUPS, 1, SSM_GROUP), (1, CHUNK, 1))
    dcol = dcol.reshape(N_SSM_GROUPS, CHUNK_F, 1)
    yst = _s5(ut.reshape(CHUNK, N_SSM_GROUPS, SSM_GROUP, n_rows), mzt, wt_state, coef, dcol, nj, n_steps)
    yst = yst.reshape(CHUNK, D_SSM, n_rows)

    tl = min(512, seq)
    c_act = _conv(vt, bsz, seq, conv_w.reshape(CONV_WIDTH, D_CONV).astype(F32),
                  row(conv_b), row(ln_c_g), row(ln_c_b), tl)

    w_r = jnp.zeros((D_MODEL, ROUTE_LANES), F32)
    w_r = w_r.at[:, :N_GROUPS].set(w_rg).at[:, N_GROUPS:N_GROUPS + N_EXPERTS].set(w_re)
    b_r = jnp.zeros((1, ROUTE_LANES), F32)
    b_r = b_r.at[0, :N_GROUPS].set(b_rg).at[0, N_GROUPS:N_GROUPS + N_EXPERTS].set(b_re)
    w_r_hi = w_r.astype(BF16)
    w_r_lo = (w_r - w_r_hi.astype(F32)).astype(BF16)
    wts = (w_in[:, n_ucv:].astype(BF16), row(b_in[n_ucv:]),
           w_glu.astype(BF16), row(b_glu),
           w_cout.astype(BF16), row(b_cout), w_out.astype(BF16), row(b_out),
           row(ln1_g), row(ln1_b), w_r_hi, w_r_lo, b_r)
    ht, routet = _mix(xt, yst, c_act, wts, rg, alpha)

    tm_moe = 128
    route = routet.reshape(n_tok, ROUTE_LANES)
    cls = route[:, 4].astype(jnp.int32)
    first_is_a = route[:, 0] < route[:, 1]
    gate_a = jnp.where(first_is_a, route[:, 2], route[:, 3])
    gate_b = jnp.where(first_is_a, route[:, 3], route[:, 2])
    blk_ea, blk_eb, blk_groups, tok_rows, gate_rows, pos = _dispatch(cls, gate_a, gate_b, tm_moe)
    w_gu = jnp.concatenate([w_gate, w_up], axis=-1).astype(BF16)
    y_buf = _moe(blk_ea, blk_eb, blk_groups, tok_rows, ht.reshape(n_tok * TOKEN_SUBLANES, LANES), gate_rows,
                 w_gu, w_down.astype(BF16), tm_moe)
    tt = CHUNK * CHUNK
    n_r = tt // CHUNK
    pos_rows = pos.reshape(CHUNK, n_rows // n_r, n_r).transpose(1, 0, 2).reshape(n_tok)
    out = _combine(pos_rows, ht, y_buf, row(ln2_g), row(ln2_b), tt, alpha)
    return out.reshape(bsz, seq, D_MODEL)


def kernel(x, w_in, b_in, lam_re, lam_im, log_dt, ssm_b_re, ssm_b_im, ssm_c_re, ssm_c_im, ssm_d, w_glu, b_glu, conv_w, conv_b, ln_c_g, ln_c_b, w_cout, b_cout, w_out, b_out, ln1_g, ln1_b, w_route_group, b_route_group, w_route_expert, b_route_expert, w_gate, w_up, w_down, ln2_g, ln2_b):
    params = (w_in, b_in, lam_re, lam_im, log_dt, ssm_b_re, ssm_b_im, ssm_c_re, ssm_c_im,
              ssm_d, w_glu, b_glu, conv_w, conv_b, ln_c_g, ln_c_b, w_cout, b_cout, w_out, b_out,
              ln1_g, ln1_b, w_route_group, b_route_group, w_route_expert, b_route_expert,
              w_gate, w_up, w_down, ln2_g, ln2_b)
    depth = w_in.shape[0]
    alpha = (2.0 * depth) ** 0.25
    h = x
    for i in range(depth):
        h = _layer(h, tuple(p[i] for p in params), alpha)
    return h
```
